```python
import math
import jax, jax.numpy as jnp
from jax import lax
import numpy as np

D_MODEL = 2048
BATCH = 1
SEQ = 8192
DEPTH = 1

CHUNK = 64
Q_BLOCK = 128
MIX_WIDTH = D_MODEL
DA_HEADS = 8
DA_QK_DIM = 64
DA_V_DIM = 2 * DA_QK_DIM
DA_QK_WIDTH = DA_HEADS * 2 * DA_QK_DIM
DA_WIDTH = DA_HEADS * DA_V_DIM
HG_HEADS = 8
HG_WIDTH = MIX_WIDTH - DA_WIDTH
HG_V_DIM = HG_WIDTH // HG_HEADS
HG_EXPAND = 128
HG_F_WIDTH = HG_HEADS * HG_EXPAND
IN_SPLITS = (DA_QK_WIDTH, DA_QK_WIDTH, DA_WIDTH, HG_F_WIDTH, HG_F_WIDTH, HG_WIDTH, HG_WIDTH)
IN_COLS = sum(IN_SPLITS)
ROPE_THETA = 10000.0
N_EXPERTS = 256
TOP_K = 8
N_GROUPS = 8
TOPK_GROUPS = 4
EXPERT_DIM = 512
SHARED_DIM = 512
ROUTED_SCALE = 2.5
MOE_BLOCK = 64
ALPHA = (2.0 * DEPTH) ** 0.25
BETA = (8.0 * DEPTH) ** -0.25
LN_EPS = 1e-5
RMS_EPS = 1e-5

kernel_name = "hymba_diffattn_hgrn2_moe_deepnorm"


def _layer_norm(x, g, b):
    xf = x.astype(jnp.float32)
    mu = xf.mean(-1, keepdims=True)
    var = jnp.square(xf - mu).mean(-1, keepdims=True)
    return ((xf - mu) * lax.rsqrt(var + LN_EPS) * g.astype(jnp.float32) + b.astype(jnp.float32)).astype(x.dtype)


def _rms_norm(x, g):
    xf = x.astype(jnp.float32)
    return xf * lax.rsqrt(jnp.mean(xf * xf, -1, keepdims=True) + RMS_EPS) * g.astype(jnp.float32)


def _rope_tables(positions):
    inv = ROPE_THETA ** (-jnp.arange(0, DA_QK_DIM, 2, dtype=jnp.float32) / DA_QK_DIM)
    ang = positions.astype(jnp.float32)[..., None] * inv
    ang = jnp.concatenate([ang, ang], axis=-1)
    return jnp.cos(ang), jnp.sin(ang)


def _apply_rope(x, cos, sin):
    half = x.shape[-1] // 2
    rot = jnp.concatenate([-x[..., half:], x[..., :half]], axis=-1)
    return x * cos + rot * sin


def _diff_attention(q, k, v, lq1, lk1, lq2, lk2, norm_g, cos, sin, lambda_init):
    B, S = q.shape[:2]
    q = q.reshape(B, S, DA_HEADS, 2, DA_QK_DIM).transpose(0, 2, 3, 1, 4).astype(jnp.float32)
    k = k.reshape(B, S, DA_HEADS, 2, DA_QK_DIM).transpose(0, 2, 3, 1, 4).astype(jnp.float32)
    v = v.reshape(B, S, DA_HEADS, DA_V_DIM).transpose(0, 2, 1, 3)
    c, s = cos[:, None, None], sin[:, None, None]
    q = _apply_rope(q, c, s) * (DA_QK_DIM ** -0.5)
    k = _apply_rope(k, c, s)
    f32 = jnp.float32
    lam = (jnp.exp(jnp.sum(lq1.astype(f32) * lk1.astype(f32)))
           - jnp.exp(jnp.sum(lq2.astype(f32) * lk2.astype(f32))) + lambda_init)
    n_blk = S // Q_BLOCK
    key_chunk = jnp.arange(S) // CHUNK
    q_blocks = q.reshape(B, DA_HEADS, 2, n_blk, Q_BLOCK, DA_QK_DIM).transpose(3, 0, 1, 2, 4, 5)

    def one_block(args):
        qb, b_idx = args
        scores = jnp.einsum('bhmqd,bhmkd->bhmqk', qb, k)
        q_chunk = (b_idx * Q_BLOCK + jnp.arange(Q_BLOCK)) // CHUNK
        mask = key_chunk[None, :] <= q_chunk[:, None]
        probs = jax.nn.softmax(jnp.where(mask, scores, -jnp.inf), axis=-1)
        wts = probs[:, :, 0] - lam * probs[:, :, 1]
        return jnp.einsum('bhqk,bhkv->bhqv', wts.astype(v.dtype), v)

    out = lax.map(one_block, (q_blocks, jnp.arange(n_blk)))
    out = out.transpose(1, 0, 3, 2, 4).reshape(B, S, DA_HEADS, DA_V_DIM)
    out = _rms_norm(out, norm_g) * (1.0 - lambda_init)
    return out.reshape(B, S, DA_WIDTH)


def _hgrn2(q, f_logit, i, g, lb, norm_g):
    B, S = q.shape[:2]
    f32 = jnp.float32
    q = jax.nn.silu(q.astype(f32)).reshape(B, S, HG_HEADS, HG_EXPAND)
    lb = lb.reshape(HG_HEADS, HG_EXPAND)
    f = lb + (1.0 - lb) * jax.nn.sigmoid(f_logit.astype(f32).reshape(B, S, HG_HEADS, HG_EXPAND))
    kk = 1.0 - f
    log_f = jnp.log(f)
    iv = i.astype(f32).reshape(B, S, HG_HEADS, HG_V_DIM)
    n_chunks = S // CHUNK

    def to_chunks(t):
        return t.reshape(B, n_chunks, CHUNK, HG_HEADS, t.shape[-1]).transpose(1, 0, 3, 2, 4)

    causal = jnp.arange(CHUNK)[:, None] >= jnp.arange(CHUNK)[None, :]

    def step(state, xs):
        qc, kc, vc, lfc = xs
        b = jnp.cumsum(lfc, axis=2)
        b_last = b[:, :, -1:, :]
        o_inter = jnp.einsum('bhtd,bhdv->bhtv', qc * jnp.exp(b), state)
        rel = b[:, :, :, None, :] - b[:, :, None, :, :]
        decay = jnp.exp(jnp.where(causal[:, :, None], rel, -jnp.inf))
        attn = jnp.einsum('bhtd,bhsd,bhtsd->bhts', qc, kc, decay)
        o_intra = jnp.einsum('bhts,bhsv->bhtv', attn, vc)
        new_state = (jnp.exp(b_last[:, :, 0, :])[..., None] * state
                     + jnp.einsum('bhsd,bhsv->bhdv', kc * jnp.exp(b_last - b), vc))
        return new_state, o_inter + o_intra

    s0 = jnp.zeros((B, HG_HEADS, HG_EXPAND, HG_V_DIM), f32)
    _, out = lax.scan(step, s0, (to_chunks(q), to_chunks(kk), to_chunks(iv), to_chunks(log_f)))
    out = out.transpose(1, 0, 3, 2, 4).reshape(B, S, HG_HEADS, HG_V_DIM)
    gate = jax.nn.silu(g.astype(f32)).reshape(B, S, HG_HEADS, HG_V_DIM)
    out = _rms_norm(out, norm_g) * gate
    return out.reshape(B, S, HG_WIDTH)


def _moe(x, w_router, e_bias, w_gate, w_up, w_down, ws_gate, ws_up, ws_down):
    N, D = x.shape
    f32 = jnp.float32
    scores = jax.nn.sigmoid(x.astype(f32) @ w_router.astype(f32))
    choice = scores + e_bias.astype(f32)
    grp_scores = lax.top_k(choice.reshape(N, N_GROUPS, N_EXPERTS // N_GROUPS), 2)[0].sum(-1)
    _, g_idx = lax.top_k(grp_scores, TOPK_GROUPS)
    g_mask = jnp.any(g_idx[..., None] == jnp.arange(N_GROUPS), axis=-2)
    e_mask = jnp.repeat(g_mask, N_EXPERTS // N_GROUPS, axis=-1)
    _, e_idx = lax.top_k(jnp.where(e_mask, choice, -jnp.inf), TOP_K)
    w = jnp.take_along_axis(scores, e_idx, axis=-1)
    w = w / jnp.sum(w, -1, keepdims=True) * ROUTED_SCALE

    A = N * TOP_K
    flat_e = e_idx.reshape(-1)
    flat_t = jnp.repeat(jnp.arange(N, dtype=jnp.int32), TOP_K)
    flat_w = w.reshape(-1)
    order = jnp.argsort(flat_e)
    se = flat_e[order]
    counts = jax.ops.segment_sum(jnp.ones_like(flat_e), flat_e, num_segments=N_EXPERTS)
    pcounts = (counts + MOE_BLOCK - 1) // MOE_BLOCK * MOE_BLOCK
    pends = jnp.cumsum(pcounts)
    pstarts = pends - pcounts
    starts = jnp.cumsum(counts) - counts
    dest = pstarts[se] + jnp.arange(A) - starts[se]
    R = (A + N_EXPERTS * (MOE_BLOCK - 1) + MOE_BLOCK - 1) // MOE_BLOCK * MOE_BLOCK
    NB = R // MOE_BLOCK
    row_tok = jnp.full((R,), N, jnp.int32).at[dest].set(flat_t[order])
    row_w = jnp.zeros((R,), f32).at[dest].set(flat_w[order])
    blk_e = jnp.clip(jnp.searchsorted(pends, jnp.arange(NB) * MOE_BLOCK, side='right'), 0, N_EXPERTS - 1)
    x_pad = jnp.concatenate([x, jnp.zeros((1, D), x.dtype)], axis=0)

    def run_block(args):
        toks, wts, e = args
        xb = x_pad[toks]
        hdn = jax.nn.silu(xb @ w_gate[e]) * (xb @ w_up[e])
        return (hdn @ w_down[e]).astype(f32) * wts[:, None]

    y_rows = lax.map(run_block, (row_tok.reshape(NB, MOE_BLOCK), row_w.reshape(NB, MOE_BLOCK), blk_e))
    routed = jax.ops.segment_sum(y_rows.reshape(R, D), row_tok, num_segments=N + 1)[:N]
    shared = (jax.nn.silu(x @ ws_gate) * (x @ ws_up)) @ ws_down
    return (routed + shared.astype(f32)).astype(x.dtype)


def setup_inputs(seed: int = 0) -> dict:
    key = jax.random.key(seed)
    ks = jax.random.split(key, 24)
    f32 = jnp.float32
    nrm = lambda k, shp, sc: jax.random.normal(k, shp, f32) * sc
    x = jax.random.normal(ks[0], (BATCH, SEQ, D_MODEL), f32)
    offset = jax.random.randint(ks[1], (BATCH, 1), 0, 4096, dtype=jnp.int32)
    positions = offset + jnp.arange(SEQ, dtype=jnp.int32)[None, :]
    col_scale = jnp.asarray(np.concatenate([
        np.ones(DA_QK_WIDTH * 2), np.full(DA_WIDTH, BETA),
        np.ones(HG_F_WIDTH * 2), np.full(HG_WIDTH, BETA), np.ones(HG_WIDTH)]).astype(np.float32))
    w_in = nrm(ks[2], (DEPTH, D_MODEL, IN_COLS), D_MODEL ** -0.5) * col_scale
    w_out = nrm(ks[3], (DEPTH, MIX_WIDTH, D_MODEL), MIX_WIDTH ** -0.5 * BETA)
    return {
        "x": x,
        "positions": positions,
        "ln_in_g": 1.0 + nrm(ks[4], (D_MODEL,), 0.02),
        "ln_in_b": nrm(ks[5], (D_MODEL,), 0.02),
        "w_in": w_in,
        "w_out": w_out,
        "lam_q1": nrm(ks[6], (DEPTH, DA_QK_DIM), 0.1),
        "lam_k1": nrm(ks[7], (DEPTH, DA_QK_DIM), 0.1),
        "lam_q2": nrm(ks[8], (DEPTH, DA_QK_DIM), 0.1),
        "lam_k2": nrm(ks[9], (DEPTH, DA_QK_DIM), 0.1),
        "da_norm_g": 1.0 + nrm(ks[10], (DEPTH, DA_V_DIM), 0.02),
        "hg_lb_logits": nrm(ks[11], (DEPTH + 1, HG_F_WIDTH), 0.1) + jnp.arange(DEPTH + 1, dtype=f32)[:, None],
        "hg_norm_g": 1.0 + nrm(ks[12], (DEPTH, HG_V_DIM), 0.02),
        "ln1_g": 1.0 + nrm(ks[13], (DEPTH, D_MODEL), 0.02),
        "ln1_b": nrm(ks[14], (DEPTH, D_MODEL), 0.02),
        "w_router": nrm(ks[15], (DEPTH, D_MODEL, N_EXPERTS), D_MODEL ** -0.5),
        "e_bias": nrm(ks[16], (DEPTH, N_EXPERTS), 0.01),
        "w_gate": nrm(ks[17], (DEPTH, N_EXPERTS, D_MODEL, EXPERT_DIM), D_MODEL ** -0.5),
        "w_up": nrm(ks[18], (DEPTH, N_EXPERTS, D_MODEL, EXPERT_DIM), D_MODEL ** -0.5),
        "w_down": nrm(ks[19], (DEPTH, N_EXPERTS, EXPERT_DIM, D_MODEL), EXPERT_DIM ** -0.5 * BETA),
        "ws_gate": nrm(ks[20], (DEPTH, D_MODEL, SHARED_DIM), D_MODEL ** -0.5),
        "ws_up": nrm(ks[21], (DEPTH, D_MODEL, SHARED_DIM), D_MODEL ** -0.5),
        "ws_down": nrm(ks[22], (DEPTH, SHARED_DIM, D_MODEL), SHARED_DIM ** -0.5 * BETA),
        "ln2_g": 1.0 + nrm(ks[23], (DEPTH, D_MODEL), 0.02),
        "ln2_b": nrm(jax.random.fold_in(ks[23], 1), (DEPTH, D_MODEL), 0.02),
    }


def reference(x, positions, ln_in_g, ln_in_b, w_in, w_out, lam_q1, lam_k1, lam_q2, lam_k2, da_norm_g,
              hg_lb_logits, hg_norm_g, ln1_g, ln1_b, w_router, e_bias, w_gate, w_up, w_down,
              ws_gate, ws_up, ws_down, ln2_g, ln2_b):
    B, S, D = x.shape
    cos, sin = _rope_tables(positions)
    lower_bounds = jnp.cumsum(jax.nn.softmax(hg_lb_logits.astype(jnp.float32), axis=0), axis=0)
    split_at = [int(v) for v in np.cumsum(IN_SPLITS)[:-1]]
    h = _layer_norm(x, ln_in_g, ln_in_b)
    for layer in range(DEPTH):
        lambda_init = 0.8 - 0.6 * math.exp(-0.3 * layer)
        proj = h @ w_in[layer]
        qa, ka, va, qh, fh, ih, gh = jnp.split(proj, split_at, axis=-1)
        y_a = _diff_attention(qa, ka, va, lam_q1[layer], lam_k1[layer], lam_q2[layer], lam_k2[layer],
                              da_norm_g[layer], cos, sin, lambda_init)
        y_b = _hgrn2(qh, fh, ih, gh, lower_bounds[layer], hg_norm_g[layer])
        mix = jnp.concatenate([y_a.astype(h.dtype), y_b.astype(h.dtype)], axis=-1) @ w_out[layer]
        h = _layer_norm(ALPHA * h + mix, ln1_g[layer], ln1_b[layer])
        ff = _moe(h.reshape(B * S, D), w_router[layer], e_bias[layer], w_gate[layer], w_up[layer],
                  w_down[layer], ws_gate[layer], ws_up[layer], ws_down[layer]).reshape(B, S, D)
        h = _layer_norm(ALPHA * h + ff, ln2_g[layer], ln2_b[layer])
    return h
```

```python
import functools
import math

import jax
import jax.numpy as jnp
from jax import lax
from jax.experimental import pallas as pl
from jax.experimental.pallas import tpu as pltpu

F32 = jnp.float32
BF16 = jnp.bfloat16

CHUNK = 64
DA_HEADS = 8
DA_QK_DIM = 64
DA_V_DIM = 2 * DA_QK_DIM
HG_HEADS = 8
HG_DIM = 128
ROPE_THETA = 10000.0
N_EXPERTS = 256
TOP_K = 8
N_GROUPS = 8
TOPK_GROUPS = 4
GROUP_SIZE = N_EXPERTS // N_GROUPS
ROUTED_SCALE = 2.5
DEPTH = 1
ALPHA = (2.0 * DEPTH) ** 0.25
LN_EPS = 1e-5
RMS_EPS = 1e-5
LAMBDA_INIT = 0.8 - 0.6 * math.exp(-0.3 * 0)

LANES = 128
VMEM_LIMIT = 56 * 1024 * 1024
NEG = -1e30

INPROJ_TM, INPROJ_TN = 1024, 512
ROPE_TR = 512
ATTN_T = 512
HG_TR = 512
HG_SUB = 16
HG_GRP = 128
OUT_TM, OUT_TK = 512, 512
ROUTER_TM = 256
MOE_TB = 256
DISPATCH_TT = 256
COMBINE_TT = 128
SHARED_TM = 512


def _cparams(sem):
    return pltpu.CompilerParams(dimension_semantics=sem, vmem_limit_bytes=VMEM_LIMIT)


def _layer_norm(xf, g, b):
    mu = jnp.mean(xf, axis=-1, keepdims=True)
    xc = xf - mu
    var = jnp.mean(xc * xc, axis=-1, keepdims=True)
    return xc * lax.rsqrt(var + LN_EPS) * g + b


def _silu(x):
    return x * jax.nn.sigmoid(x)


def _inproj_kernel(x_ref, g_ref, b_ref, w_ref, o_ref, hb_ref):
    @pl.when(pl.program_id(1) == 0)
    def _():
        hb_ref[...] = _layer_norm(x_ref[...], g_ref[...], b_ref[...]).astype(BF16)

    o_ref[...] = jnp.dot(hb_ref[...], w_ref[...].astype(BF16), preferred_element_type=F32)


def _inproj(x, g, b, w):
    s, d = x.shape
    n = w.shape[1]
    tm, tn = INPROJ_TM, INPROJ_TN
    return pl.pallas_call(
        _inproj_kernel,
        grid=(s // tm, n // tn),
        in_specs=[
            pl.BlockSpec((tm, d), lambda i, j: (i, 0)),
            pl.BlockSpec((1, d), lambda i, j: (0, 0)),
            pl.BlockSpec((1, d), lambda i, j: (0, 0)),
            pl.BlockSpec((d, tn), lambda i, j: (0, j)),
        ],
        out_specs=pl.BlockSpec((tm, tn), lambda i, j: (i, j)),
        out_shape=jax.ShapeDtypeStruct((s, n), F32),
        scratch_shapes=[pltpu.VMEM((tm, d), BF16)],
        compiler_params=_cparams(("arbitrary", "arbitrary")),
        name="inproj",
    )(x, g, b, w)


def _rope_kernel(q_ref, k_ref, v_ref, cos_ref, sin_ref, qm_ref, kr_ref, vb_ref):
    cos = cos_ref[...]
    sin = sin_ref[...]
    lane = lax.broadcasted_iota(jnp.int32, cos.shape, 1)
    first_half = (lane % DA_QK_DIM) < (DA_QK_DIM // 2)

    def rope(x):
        rot = jnp.where(first_half, pltpu.roll(x, LANES - DA_QK_DIM // 2, 1), pltpu.roll(x, DA_QK_DIM // 2, 1))
        return x * cos + rot * sin

    q = rope(q_ref[...]) * (DA_QK_DIM ** -0.5)
    k = rope(k_ref[...])
    map0 = lane < DA_QK_DIM
    qm_ref[0, 0] = jnp.where(map0, q, 0.0).astype(BF16)
    qm_ref[0, 1] = jnp.where(map0, 0.0, q).astype(BF16)
    kr_ref[0] = k.astype(BF16)
    vb_ref[0] = v_ref[...].astype(BF16)


def _rope(proj, cos, sin):
    s = proj.shape[0]
    tr = ROPE_TR
    h = DA_HEADS
    blk = lambda off: pl.BlockSpec((tr, LANES), lambda i, hh: (i, off + hh))
    tab = pl.BlockSpec((tr, LANES), lambda i, hh: (i, 0))
    return pl.pallas_call(
        _rope_kernel,
        grid=(s // tr, h),
        in_specs=[blk(0), blk(h), blk(2 * h), tab, tab],
        out_specs=[
            pl.BlockSpec((1, 2, tr, LANES), lambda i, hh: (hh, 0, i, 0)),
            pl.BlockSpec((1, tr, LANES), lambda i, hh: (hh, i, 0)),
            pl.BlockSpec((1, tr, LANES), lambda i, hh: (hh, i, 0)),
        ],
        out_shape=[
            jax.ShapeDtypeStruct((h, 2, s, LANES), BF16),
            jax.ShapeDtypeStruct((h, s, LANES), BF16),
            jax.ShapeDtypeStruct((h, s, LANES), BF16),
        ],
        compiler_params=_cparams(("arbitrary", "arbitrary")),
        name="rope",
    )(proj, proj, proj, cos, sin)


def _attn_kernel(lam_ref, q_ref, k_ref, v_ref, g_ref, o_ref, acc_ref, m_ref, l_ref):
    t = ATTN_T
    i = pl.program_id(1)
    q = q_ref[0].reshape(2 * t, LANES)
    m_ref[...] = jnp.full(m_ref.shape, NEG, F32)
    l_ref[...] = jnp.zeros(l_ref.shape, F32)
    acc_ref[...] = jnp.zeros(acc_ref.shape, F32)

    def step(c, masked):
        start = pl.multiple_of(c * t, t)
        k = k_ref[0, pl.ds(start, t), :]
        v = v_ref[0, pl.ds(start, t), :]
        s = lax.dot_general(q, k, (((1,), (1,)), ((), ())), preferred_element_type=F32)
        if masked:
            row = lax.broadcasted_iota(jnp.int32, s.shape, 0) % t
            col = lax.broadcasted_iota(jnp.int32, s.shape, 1)
            s = jnp.where((col // CHUNK) <= (row // CHUNK), s, NEG)
        m_prev = m_ref[...]
        m_new = jnp.maximum(m_prev, jnp.max(s, axis=-1, keepdims=True))
        alpha = jnp.exp(m_prev - m_new)
        p = jnp.exp(s - m_new)
        l_ref[...] = alpha * l_ref[...] + jnp.sum(p, axis=-1, keepdims=True)
        acc_ref[...] = alpha * acc_ref[...] + jnp.dot(p.astype(BF16), v, preferred_element_type=F32)
        m_ref[...] = m_new

    def body(c, carry):
        step(c, False)
        return carry

    lax.fori_loop(0, i, body, 0)
    step(i, True)

    lam = lam_ref[0]
    acc = acc_ref[...]
    l = l_ref[...]
    o = acc[:t] / l[:t] - lam * (acc[t:] / l[t:])
    o = o * lax.rsqrt(jnp.mean(o * o, axis=-1, keepdims=True) + RMS_EPS) * g_ref[...] * (1.0 - LAMBDA_INIT)
    o_ref[...] = o.astype(o_ref.dtype)


def _attention(lam, qm, kr, vb, norm_g):
    h, _, s, _ = qm.shape
    t = ATTN_T
    return pl.pallas_call(
        _attn_kernel,
        grid=(h, s // t),
        in_specs=[
            pl.BlockSpec(memory_space=pltpu.SMEM),
            pl.BlockSpec((1, 2, t, LANES), lambda hh, i: (hh, 0, i, 0)),
            pl.BlockSpec((1, s, LANES), lambda hh, i: (hh, 0, 0)),
            pl.BlockSpec((1, s, LANES), lambda hh, i: (hh, 0, 0)),
            pl.BlockSpec((1, LANES), lambda hh, i: (0, 0)),
        ],
        out_specs=pl.BlockSpec((t, LANES), lambda hh, i: (i, hh)),
        out_shape=jax.ShapeDtypeStruct((s, h * LANES), BF16),
        scratch_shapes=[
            pltpu.VMEM((2 * t, LANES), F32),
            pltpu.VMEM((2 * t, 1), F32),
            pltpu.VMEM((2 * t, 1), F32),
        ],
        compiler_params=_cparams(("arbitrary", "arbitrary")),
        name="diff_attn",
    )(lam, qm, kr, vb, norm_g)


def _hgrn_kernel(q_ref, f_ref, i_ref, g_ref, lb_ref, ng_ref, o_ref,
                 st_ref, qs_ref, kk_ref, b_ref, kh_ref, od_ref):
    tr, sub, grp = HG_TR, HG_SUB, HG_GRP

    @pl.when(pl.program_id(1) == 0)
    def _():
        st_ref[...] = jnp.zeros(st_ref.shape, F32)

    lb = lb_ref[...]
    f = lb + (1.0 - lb) * jax.nn.sigmoid(f_ref[...])
    log_f = jnp.log(f)
    kk = 1.0 - f
    qs_ref[...] = _silu(q_ref[...])
    kk_ref[...] = kk

    r = lax.broadcasted_iota(jnp.int32, (grp, grp), 0)
    c = lax.broadcasted_iota(jnp.int32, (grp, grp), 1)
    same = (r // sub) == (c // sub)
    tri = jnp.where(same & (c <= r), 1.0, 0.0).astype(F32)
    blk = jnp.where(same, 1.0, 0.0).astype(F32)
    for gi in range(tr // grp):
        rows = slice(gi * grp, (gi + 1) * grp)
        lf = log_f[rows]
        bg = jnp.dot(tri, lf, preferred_element_type=F32, precision=lax.Precision.HIGHEST)
        tot = jnp.dot(blk, lf, preferred_element_type=F32, precision=lax.Precision.HIGHEST)
        b_ref[rows, :] = bg
        kh_ref[rows, :] = kk[rows] * jnp.exp(tot - bg)

    sub_iota = lax.broadcasted_iota(jnp.int32, (sub, LANES), 0)
    grp_iota = lax.broadcasted_iota(jnp.int32, (grp, LANES), 0)

    def group_body(gi, carry):
        gbase = pl.multiple_of(gi * grp, grp)
        iv_t = i_ref[pl.ds(gbase, grp), :].T.astype(BF16)
        kh_g = kh_ref[pl.ds(gbase, grp), :]
        for j in range(grp // sub):
            base = gbase + j * sub
            bj = b_ref[pl.ds(base, sub), :]
            qj = qs_ref[pl.ds(base, sub), :]
            kj = kk_ref[pl.ds(base, sub), :]
            ivj = i_ref[pl.ds(base, sub), :]
            st = st_ref[...]
            o_inter = lax.dot_general((qj * jnp.exp(bj)).astype(BF16), st.astype(BF16),
                                      (((1,), (1,)), ((), ())), preferred_element_type=F32)
            o_sub = o_inter
            for tt in range(sub):
                rel = bj[tt:tt + 1] - bj
                e = jnp.exp(jnp.where(sub_iota <= tt, rel, NEG))
                w = jnp.sum(qj[tt:tt + 1] * kj * e, axis=-1, keepdims=True)
                o_row = jnp.sum(w * ivj, axis=0, keepdims=True)
                o_sub = jnp.where(sub_iota == tt, o_sub + o_row, o_sub)
            od_ref[pl.ds(base, sub), :] = o_sub
            in_sub = (grp_iota >= j * sub) & (grp_iota < (j + 1) * sub)
            kh_j = jnp.where(in_sub, kh_g, 0.0).astype(BF16)
            upd = jnp.dot(iv_t, kh_j, preferred_element_type=F32)
            st_ref[...] = st * jnp.exp(bj[sub - 1:sub]) + upd
        return carry

    lax.fori_loop(0, tr // grp, group_body, 0)

    o = od_ref[...]
    o = o * lax.rsqrt(jnp.mean(o * o, axis=-1, keepdims=True) + RMS_EPS) * ng_ref[...]
    o_ref[...] = (o * _silu(g_ref[...])).astype(o_ref.dtype)


def _hgrn(proj, lower_bound, norm_g, col0):
    s = proj.shape[0]
    tr, h = HG_TR, HG_HEADS
    blk = lambda off: pl.BlockSpec((tr, LANES), lambda hh, i: (i, col0 + off + hh))
    return pl.pallas_call(
        _hgrn_kernel,
        grid=(h, s // tr),
        in_specs=[blk(0), blk(h), blk(2 * h), blk(3 * h),
                  pl.BlockSpec((1, LANES), lambda hh, i: (0, hh)),
                  pl.BlockSpec((1, LANES), lambda hh, i: (0, 0))],
        out_specs=pl.BlockSpec((tr, LANES), lambda hh, i: (i, hh)),
        out_shape=jax.ShapeDtypeStruct((s, h * LANES), BF16),
        scratch_shapes=[pltpu.VMEM((HG_DIM, HG_DIM), F32)] + [pltpu.VMEM((tr, LANES), F32)] * 5,
        compiler_params=_cparams(("arbitrary", "arbitrary")),
        name="hgrn2",
    )(proj, proj, proj, proj, lower_bound, norm_g)


def _outproj_kernel(a_ref, w_ref, x_ref, gi_ref, bi_ref, g1_ref, b1_ref, h_ref, acc_ref):
    k = pl.program_id(1)

    @pl.when(k == 0)
    def _():
        acc_ref[...] = jnp.zeros(acc_ref.shape, F32)

    acc_ref[...] += jnp.dot(a_ref[...], w_ref[...].astype(BF16), preferred_element_type=F32)

    @pl.when(k == pl.num_programs(1) - 1)
    def _():
        h0 = _layer_norm(x_ref[...], gi_ref[...], bi_ref[...])
        h_ref[...] = _layer_norm(ALPHA * h0 + acc_ref[...], g1_ref[...], b1_ref[...])


def _outproj(a, w, x, gi, bi, g1, b1):
    s, d = x.shape
    kdim = a.shape[1]
    tm, tk = OUT_TM, OUT_TK
    vec = pl.BlockSpec((1, d), lambda i, k: (0, 0))
    return pl.pallas_call(
        _outproj_kernel,
        grid=(s // tm, kdim // tk),
        in_specs=[
            pl.BlockSpec((tm, tk), lambda i, k: (i, k)),
            pl.BlockSpec((tk, d), lambda i, k: (k, 0)),
            pl.BlockSpec((tm, d), lambda i, k: (i, 0)),
            vec, vec, vec, vec,
        ],
        out_specs=pl.BlockSpec((tm, d), lambda i, k: (i, 0)),
        out_shape=jax.ShapeDtypeStruct((s, d), F32),
        scratch_shapes=[pltpu.VMEM((tm, d), F32)],
        compiler_params=_cparams(("arbitrary", "arbitrary")),
        name="outproj_ln1",
    )(a, w, x, gi, bi, g1, b1)


def _router_kernel(h_ref, w_ref, eb_ref, idx_ref, wt_ref, rank_ref, cnt_ref, carry_ref):
    tm = ROUTER_TM
    i = pl.program_id(0)

    @pl.when(i == 0)
    def _():
        carry_ref[...] = jnp.zeros(carry_ref.shape, F32)

    logits = jnp.dot(h_ref[...], w_ref[...], preferred_element_type=F32, precision=lax.Precision.HIGHEST)
    scores = jax.nn.sigmoid(logits)
    choice = scores + eb_ref[...]
    lane = lax.broadcasted_iota(jnp.int32, choice.shape, 1)
    grp_of_lane = lane // GROUP_SIZE
    big = jnp.int32(N_EXPERTS)
    ninf = -jnp.inf

    def first_argmax(vals):
        m = jnp.max(vals, axis=-1, keepdims=True)
        idx = jnp.min(jnp.where(vals == m, lane, big), axis=-1, keepdims=True)
        return m, idx

    gs = []
    for g in range(N_GROUPS):
        vals = jnp.where(grp_of_lane == g, choice, ninf)
        m1, i1 = first_argmax(vals)
        m2 = jnp.max(jnp.where(lane == i1, ninf, vals), axis=-1, keepdims=True)
        gs.append(m1 + m2)
    allowed = jnp.zeros(choice.shape, jnp.int32)
    for g in range(N_GROUPS):
        beats = jnp.zeros(gs[g].shape, jnp.int32)
        for g2 in range(N_GROUPS):
            if g2 == g:
                continue
            better = (gs[g2] >= gs[g]) if g2 < g else (gs[g2] > gs[g])
            beats = beats + jnp.where(better, 1, 0)
        allowed = jnp.where(grp_of_lane == g, jnp.where(beats < TOPK_GROUPS, 1, 0), allowed)
    masked = jnp.where(allowed > 0, choice, ninf)

    sel = jnp.zeros(choice.shape, F32)
    idxs, wts = [], []
    for _ in range(TOP_K):
        _, ik = first_argmax(masked)
        hit = lane == ik
        idxs.append(ik)
        wts.append(jnp.sum(jnp.where(hit, scores, 0.0), axis=-1, keepdims=True))
        sel = jnp.where(hit, 1.0, sel)
        masked = jnp.where(hit, ninf, masked)
    wsum = wts[0]
    for k in range(1, TOP_K):
        wsum = wsum + wts[k]

    r = lax.broadcasted_iota(jnp.int32, (tm, tm), 0)
    c = lax.broadcasted_iota(jnp.int32, (tm, tm), 1)
    strict_lower = jnp.where(c < r, 1.0, 0.0).astype(BF16)
    rank = jnp.dot(strict_lower, sel.astype(BF16), preferred_element_type=F32) + carry_ref[...]
    carry_ref[...] = carry_ref[...] + jnp.sum(sel, axis=0, keepdims=True)
    cnt_ref[...] = carry_ref[...]

    out_lane = lax.broadcasted_iota(jnp.int32, (tm, LANES), 1)
    idx_out = jnp.zeros((tm, LANES), jnp.int32)
    wt_out = jnp.zeros((tm, LANES), F32)
    rank_out = jnp.zeros((tm, LANES), jnp.int32)
    for k in range(TOP_K):
        rk = jnp.sum(jnp.where(lane == idxs[k], rank, 0.0), axis=-1, keepdims=True)
        idx_out = jnp.where(out_lane == k, idxs[k], idx_out)
        wt_out = jnp.where(out_lane == k, wts[k] / wsum * ROUTED_SCALE, wt_out)
        rank_out = jnp.where(out_lane == k, rk.astype(jnp.int32), rank_out)
    idx_ref[...] = idx_out
    wt_ref[...] = wt_out
    rank_ref[...] = rank_out


def _router(h, w_router, e_bias):
    s, d = h.shape
    e = w_router.shape[1]
    tm = ROUTER_TM
    row = pl.BlockSpec((tm, LANES), lambda i: (i, 0))
    return pl.pallas_call(
        _router_kernel,
        grid=(s // tm,),
        in_specs=[
            pl.BlockSpec((tm, d), lambda i: (i, 0)),
            pl.BlockSpec((d, e), lambda i: (0, 0)),
            pl.BlockSpec((1, e), lambda i: (0, 0)),
        ],
        out_specs=[row, row, row, pl.BlockSpec((1, e), lambda i: (0, 0))],
        out_shape=[
            jax.ShapeDtypeStruct((s, LANES), jnp.int32),
            jax.ShapeDtypeStruct((s, LANES), F32),
            jax.ShapeDtypeStruct((s, LANES), jnp.int32),
            jax.ShapeDtypeStruct((1, e), F32),
        ],
        scratch_shapes=[pltpu.VMEM((1, e), F32)],
        compiler_params=_cparams(("arbitrary",)),
        name="router_topk",
    )(h, w_router, e_bias)


def _dispatch_kernel(dest_ref, h_ref, xs_in_ref, xs_ref, sem):
    del xs_in_ref
    tt = DISPATCH_TT

    def row_copy(t, k):
        return pltpu.make_async_copy(h_ref.at[pl.ds(t, 1)], xs_ref.at[pl.ds(dest_ref[k, t], 1)], sem)

    def issue(t, carry):
        for k in range(TOP_K):
            row_copy(t, k).start()
        return carry

    def drain(t, carry):
        for k in range(TOP_K):
            row_copy(t, k).wait()
        return carry

    lax.fori_loop(0, tt, issue, 0)
    lax.fori_loop(0, tt, drain, 0)


def _dispatch(dest_t, h, xs_zero):
    s, d = h.shape
    tt = DISPATCH_TT
    return pl.pallas_call(
        _dispatch_kernel,
        grid=(s // tt,),
        in_specs=[
            pl.BlockSpec((TOP_K, tt), lambda i: (0, i), memory_space=pltpu.SMEM),
            pl.BlockSpec((tt, d), lambda i: (i, 0)),
            pl.BlockSpec(memory_space=pl.ANY),
        ],
        out_specs=pl.BlockSpec(memory_space=pl.ANY),
        out_shape=jax.ShapeDtypeStruct(xs_zero.shape, xs_zero.dtype),
        scratch_shapes=[pltpu.SemaphoreType.DMA(())],
        input_output_aliases={2: 0},
        compiler_params=_cparams(("arbitrary",)),
        name="moe_dispatch",
    )(dest_t, h, xs_zero)


def _expert_kernel(blk_e_ref, nb_ref, x_ref, wg_ref, wu_ref, wd_ref, y_ref, wgb_ref, wub_ref, wdb_ref):
    b = pl.program_id(0)

    @pl.when(b < nb_ref[0])
    def _():
        prev = blk_e_ref[jnp.maximum(b - 1, 0)]

        @pl.when((b == 0) | (blk_e_ref[b] != prev))
        def _():
            wgb_ref[...] = wg_ref[0].astype(BF16)
            wub_ref[...] = wu_ref[0].astype(BF16)
            wdb_ref[...] = wd_ref[0].astype(BF16)

        xb = x_ref[...].astype(BF16)
        gate = jnp.dot(xb, wgb_ref[...], preferred_element_type=F32)
        up = jnp.dot(xb, wub_ref[...], preferred_element_type=F32)
        hdn = (_silu(gate) * up).astype(BF16)
        y_ref[...] = jnp.dot(hdn, wdb_ref[...], preferred_element_type=F32)


def _experts(blk_e, nb, xs, w_gate, w_up, w_down):
    r, d = xs.shape
    _, _, f = w_gate.shape
    tb = MOE_TB
    row_blk = lambda b, be, nbr: (jnp.minimum(b, nbr[0] - 1), 0)
    wsel = lambda b, be, nbr: (be[b], 0, 0)
    grid_spec = pltpu.PrefetchScalarGridSpec(
        num_scalar_prefetch=2,
        grid=(r // tb,),
        in_specs=[
            pl.BlockSpec((tb, d), row_blk),
            pl.BlockSpec((1, d, f), wsel),
            pl.BlockSpec((1, d, f), wsel),
            pl.BlockSpec((1, f, d), wsel),
        ],
        out_specs=pl.BlockSpec((tb, d), row_blk),
        scratch_shapes=[pltpu.VMEM((d, f), BF16), pltpu.VMEM((d, f), BF16), pltpu.VMEM((f, d), BF16)],
    )
    return pl.pallas_call(
        _expert_kernel,
        grid_spec=grid_spec,
        out_shape=jax.ShapeDtypeStruct((r, d), F32),
        compiler_params=_cparams(("arbitrary",)),
        name="moe_experts",
    )(blk_e, nb, xs, w_gate, w_up, w_down)


def _shared_kernel(h_ref, wg_ref, wu_ref, wd_ref, o_ref, wgb_ref, wub_ref, wdb_ref):
    @pl.when(pl.program_id(0) == 0)
    def _():
        wgb_ref[...] = wg_ref[...].astype(BF16)
        wub_ref[...] = wu_ref[...].astype(BF16)
        wdb_ref[...] = wd_ref[...].astype(BF16)

    hb = h_ref[...].astype(BF16)
    gate = jnp.dot(hb, wgb_ref[...], preferred_element_type=F32)
    up = jnp.dot(hb, wub_ref[...], preferred_element_type=F32)
    hdn = (_silu(gate) * up).astype(BF16)
    o_ref[...] = jnp.dot(hdn, wdb_ref[...], preferred_element_type=F32)


def _shared(h, ws_gate, ws_up, ws_down):
    s, d = h.shape
    f = ws_gate.shape[1]
    tm = SHARED_TM
    return pl.pallas_call(
        _shared_kernel,
        grid=(s // tm,),
        in_specs=[
            pl.BlockSpec((tm, d), lambda i: (i, 0)),
            pl.BlockSpec((d, f), lambda i: (0, 0)),
            pl.BlockSpec((d, f), lambda i: (0, 0)),
            pl.BlockSpec((f, d), lambda i: (0, 0)),
        ],
        out_specs=pl.BlockSpec((tm, d), lambda i: (i, 0)),
        out_shape=jax.ShapeDtypeStruct((s, d), F32),
        scratch_shapes=[pltpu.VMEM((d, f), BF16), pltpu.VMEM((d, f), BF16), pltpu.VMEM((f, d), BF16)],
        compiler_params=_cparams(("arbitrary",)),
        name="moe_shared",
    )(h, ws_gate, ws_up, ws_down)


def _combine_kernel(dest_ref, y_ref, wt_ref, h_ref, sh_ref, g_ref, b_ref, o_ref, rows_ref, sem):
    tt = COMBINE_TT

    def row_copy(t, k):
        return pltpu.make_async_copy(y_ref.at[pl.ds(dest_ref[k, t], 1)], rows_ref.at[k, pl.ds(t, 1)], sem)

    def issue(t, carry):
        for k in range(TOP_K):
            row_copy(t, k).start()
        return carry

    def drain(t, carry):
        for k in range(TOP_K):
            row_copy(t, k).wait()
        return carry

    lax.fori_loop(0, tt, issue, 0)
    lax.fori_loop(0, tt, drain, 0)

    wt = wt_ref[...]
    routed = rows_ref[0] * wt[:, 0:1]
    for k in range(1, TOP_K):
        routed = routed + rows_ref[k] * wt[:, k:k + 1]
    ff = routed + sh_ref[...]
    o_ref[...] = _layer_norm(ALPHA * h_ref[...] + ff, g_ref[...], b_ref[...])


def _combine(dest_t, y, wt, h, shared, g, b):
    s, d = h.shape
    tt = COMBINE_TT
    row = pl.BlockSpec((tt, d), lambda i: (i, 0))
    vec = pl.BlockSpec((1, d), lambda i: (0, 0))
    return pl.pallas_call(
        _combine_kernel,
        grid=(s // tt,),
        in_specs=[
            pl.BlockSpec((TOP_K, tt), lambda i: (0, i), memory_space=pltpu.SMEM),
            pl.BlockSpec(memory_space=pl.ANY),
            pl.BlockSpec((tt, LANES), lambda i: (i, 0)),
            row, row, vec, vec,
        ],
        out_specs=row,
        out_shape=jax.ShapeDtypeStruct((s, d), F32),
        scratch_shapes=[pltpu.VMEM((TOP_K, tt, d), F32), pltpu.SemaphoreType.DMA(())],
        compiler_params=_cparams(("arbitrary",)),
        name="moe_combine_ln2",
    )(dest_t, y, wt, h, shared, g, b)


def _rope_tables(positions):
    half = DA_QK_DIM // 2
    inv = ROPE_THETA ** (-jnp.arange(0, DA_QK_DIM, 2, dtype=F32) / DA_QK_DIM)
    ang = positions.astype(F32)[:, None] * inv
    cos = jnp.tile(jnp.cos(ang), (1, LANES // half))
    sin = jnp.sin(ang)
    sin = jnp.tile(jnp.concatenate([-sin, sin], axis=-1), (1, LANES // DA_QK_DIM))
    return cos, sin


def _moe(h, w_router, e_bias, w_gate, w_up, w_down, ws_gate, ws_up, ws_down, ln_g, ln_b):
    s, d = h.shape
    e = w_router.shape[1]
    tb = MOE_TB
    idx, wt, rank, counts = _router(h, w_router, e_bias.reshape(1, e))
    e_idx = idx[:, :TOP_K]
    counts = counts[0].astype(jnp.int32)
    pcounts = (counts + tb - 1) // tb * tb
    pends = jnp.cumsum(pcounts)
    pstarts = pends - pcounts
    dest_t = (pstarts[e_idx] + rank[:, :TOP_K]).T
    n_rows = (s * TOP_K + e * (tb - 1) + tb - 1) // tb * tb
    n_blk = n_rows // tb
    nb = (pends[-1] // tb).astype(jnp.int32).reshape(1)
    blk_start = jnp.minimum(jnp.arange(n_blk, dtype=jnp.int32), nb[0] - 1) * tb
    blk_e = jnp.minimum(jnp.sum(pends[None, :] <= blk_start[:, None], axis=1), e - 1).astype(jnp.int32)

    xs = _dispatch(dest_t, h, jnp.zeros((n_rows, d), F32))
    y = _experts(blk_e, nb, xs, w_gate, w_up, w_down)
    shared = _shared(h, ws_gate, ws_up, ws_down)
    return _combine(dest_t, y, wt, h, shared, ln_g.reshape(1, d), ln_b.reshape(1, d))


def kernel(x, positions, ln_in_g, ln_in_b, w_in, w_out, lam_q1, lam_k1, lam_q2, lam_k2, da_norm_g,
           hg_lb_logits, hg_norm_g, ln1_g, ln1_b, w_router, e_bias, w_gate, w_up, w_down,
           ws_gate, ws_up, ws_down, ln2_g, ln2_b):
    bsz, s, d = x.shape
    assert bsz == 1 and w_in.shape[0] == DEPTH
    x2 = x.reshape(s, d)
    cos, sin = _rope_tables(positions.reshape(s))
    lower_bounds = jnp.cumsum(jax.nn.softmax(hg_lb_logits.astype(F32), axis=0), axis=0)
    lam = (jnp.exp(jnp.sum(lam_q1[0] * lam_k1[0])) - jnp.exp(jnp.sum(lam_q2[0] * lam_k2[0]))
           + LAMBDA_INIT).reshape(1).astype(F32)
    gi, bi = ln_in_g.reshape(1, d), ln_in_b.reshape(1, d)

    proj = _inproj(x2, gi, bi, w_in[0])
    qm, kr, vb = _rope(proj, cos, sin)
    y_a = _attention(lam, qm, kr, vb, da_norm_g[0].reshape(1, LANES))
    hg_col0 = 3 * DA_HEADS
    y_b = _hgrn(proj, lower_bounds[0].reshape(1, -1), hg_norm_g[0].reshape(1, LANES), hg_col0)
    mix_in = jnp.concatenate([y_a, y_b], axis=-1)
    h1 = _outproj(mix_in, w_out[0], x2, gi, bi, ln1_g[0].reshape(1, d), ln1_b[0].reshape(1, d))
    h2 = _moe(h1, w_router[0], e_bias[0], w_gate[0], w_up[0], w_down[0],
              ws_gate[0], ws_up[0], ws_down[0], ln2_g[0], ln2_b[0])
    return h2.reshape(bsz, s, d)
```

```python
import functools
import math

import jax
import jax.numpy as jnp
from jax import lax
from jax.experimental import pallas as pl
from jax.experimental.pallas import tpu as pltpu

F32 = jnp.float32
BF16 = jnp.bfloat16

CHUNK = 64
DA_HEADS = 8
DA_QK_DIM = 64
DA_V_DIM = 2 * DA_QK_DIM
HG_HEADS = 8
HG_DIM = 128
ROPE_THETA = 10000.0
N_EXPERTS = 256
TOP_K = 8
N_GROUPS = 8
TOPK_GROUPS = 4
GROUP_SIZE = N_EXPERTS // N_GROUPS
ROUTED_SCALE = 2.5
DEPTH = 1
ALPHA = (2.0 * DEPTH) ** 0.25
LN_EPS = 1e-5
RMS_EPS = 1e-5
LAMBDA_INIT = 0.8 - 0.6 * math.exp(-0.3 * 0)

LANES = 128
VMEM_LIMIT = 56 * 1024 * 1024
NEG = -1e30

INPROJ_TM, INPROJ_TN = 1024, 512
ROPE_TR = 512
ATTN_T = 1024
HG_TR = 512
HG_SUB = 16
HG_GRP = 128
OUT_TM, OUT_TK = 512, 512
ROUTER_TM = 256
MOE_RB = 256
ROW_ALIGN = 8
DISPATCH_TT = 256
COMBINE_TT = 128
SHARED_TM = 512


def _cparams(sem):
    return pltpu.CompilerParams(dimension_semantics=sem, vmem_limit_bytes=VMEM_LIMIT)


def _layer_norm(xf, g, b):
    mu = jnp.mean(xf, axis=-1, keepdims=True)
    xc = xf - mu
    var = jnp.mean(xc * xc, axis=-1, keepdims=True)
    return xc * lax.rsqrt(var + LN_EPS) * g + b


def _silu(x):
    return x * jax.nn.sigmoid(x)


def _inproj_kernel(x_ref, g_ref, b_ref, w_ref, o_ref, hb_ref):
    @pl.when(pl.program_id(1) == 0)
    def _():
        hb_ref[...] = _layer_norm(x_ref[...], g_ref[...], b_ref[...]).astype(BF16)

    o_ref[...] = jnp.dot(hb_ref[...], w_ref[...].astype(BF16), preferred_element_type=F32)


def _inproj(x, g, b, w):
    s, d = x.shape
    n = w.shape[1]
    tm, tn = INPROJ_TM, INPROJ_TN
    return pl.pallas_call(
        _inproj_kernel,
        grid=(s // tm, n // tn),
        in_specs=[
            pl.BlockSpec((tm, d), lambda i, j: (i, 0)),
            pl.BlockSpec((1, d), lambda i, j: (0, 0)),
            pl.BlockSpec((1, d), lambda i, j: (0, 0)),
            pl.BlockSpec((d, tn), lambda i, j: (0, j)),
        ],
        out_specs=pl.BlockSpec((tm, tn), lambda i, j: (i, j)),
        out_shape=jax.ShapeDtypeStruct((s, n), F32),
        scratch_shapes=[pltpu.VMEM((tm, d), BF16)],
        compiler_params=_cparams(("arbitrary", "arbitrary")),
        name="inproj",
    )(x, g, b, w)


def _rope_kernel(q_ref, k_ref, v_ref, cos_ref, sin_ref, qm_ref, kr_ref, vb_ref):
    cos = cos_ref[...]
    sin = sin_ref[...]
    lane = lax.broadcasted_iota(jnp.int32, cos.shape, 1)
    first_half = (lane % DA_QK_DIM) < (DA_QK_DIM // 2)

    def rope(x):
        rot = jnp.where(first_half, pltpu.roll(x, LANES - DA_QK_DIM // 2, 1), pltpu.roll(x, DA_QK_DIM // 2, 1))
        return x * cos + rot * sin

    q = rope(q_ref[...]) * (DA_QK_DIM ** -0.5)
    k = rope(k_ref[...])
    map0 = lane < DA_QK_DIM
    qm_ref[0, 0] = jnp.where(map0, q, 0.0).astype(BF16)
    qm_ref[0, 1] = jnp.where(map0, 0.0, q).astype(BF16)
    kr_ref[0] = k.astype(BF16)
    vb_ref[0] = v_ref[...].astype(BF16)


def _rope(proj, cos, sin):
    s = proj.shape[0]
    tr = ROPE_TR
    h = DA_HEADS
    blk = lambda off: pl.BlockSpec((tr, LANES), lambda i, hh: (i, off + hh))
    tab = pl.BlockSpec((tr, LANES), lambda i, hh: (i, 0))
    return pl.pallas_call(
        _rope_kernel,
        grid=(s // tr, h),
        in_specs=[blk(0), blk(h), blk(2 * h), tab, tab],
        out_specs=[
            pl.BlockSpec((1, 2, tr, LANES), lambda i, hh: (hh, 0, i, 0)),
            pl.BlockSpec((1, tr, LANES), lambda i, hh: (hh, i, 0)),
            pl.BlockSpec((1, tr, LANES), lambda i, hh: (hh, i, 0)),
        ],
        out_shape=[
            jax.ShapeDtypeStruct((h, 2, s, LANES), BF16),
            jax.ShapeDtypeStruct((h, s, LANES), BF16),
            jax.ShapeDtypeStruct((h, s, LANES), BF16),
        ],
        compiler_params=_cparams(("arbitrary", "arbitrary")),
        name="rope",
    )(proj, proj, proj, cos, sin)


def _attn_kernel(lam_ref, q_ref, k_ref, v_ref, g_ref, o_ref, acc_ref, m_ref, l_ref):
    t = ATTN_T
    i = pl.program_id(1)
    q = q_ref[0].reshape(2 * t, LANES)
    m_ref[...] = jnp.full(m_ref.shape, NEG, F32)
    l_ref[...] = jnp.zeros(l_ref.shape, F32)
    acc_ref[...] = jnp.zeros(acc_ref.shape, F32)

    def step(c, masked):
        start = pl.multiple_of(c * t, t)
        k = k_ref[0, pl.ds(start, t), :]
        v = v_ref[0, pl.ds(start, t), :]
        s = lax.dot_general(q, k, (((1,), (1,)), ((), ())), preferred_element_type=F32)
        if masked:
            row = lax.broadcasted_iota(jnp.int32, s.shape, 0) % t
            col = lax.broadcasted_iota(jnp.int32, s.shape, 1)
            s = jnp.where((col // CHUNK) <= (row // CHUNK), s, NEG)
        cols = [s[:, j * LANES:(j + 1) * LANES] for j in range(t // LANES)]
        mx = cols[0]
        for cj in cols[1:]:
            mx = jnp.maximum(mx, cj)
        m_prev = m_ref[...]
        m_new = jnp.maximum(m_prev, jnp.max(mx, axis=-1, keepdims=True))
        alpha = jnp.exp(m_prev - m_new)
        ps = [jnp.exp(cj - m_new) for cj in cols]
        lsum = ps[0]
        for pj in ps[1:]:
            lsum = lsum + pj
        l_ref[...] = alpha * l_ref[...] + lsum
        m_ref[...] = m_new
        p = jnp.concatenate([pj.astype(BF16) for pj in ps], axis=-1)
        acc_ref[...] = alpha * acc_ref[...] + jnp.dot(p, v, preferred_element_type=F32)

    def body(c, carry):
        step(c, False)
        return carry

    lax.fori_loop(0, i, body, 0)
    step(i, True)

    lam = lam_ref[0]
    acc = acc_ref[...]
    l = jnp.sum(l_ref[...], axis=-1, keepdims=True)
    o = acc[:t] / l[:t] - lam * (acc[t:] / l[t:])
    o = o * lax.rsqrt(jnp.mean(o * o, axis=-1, keepdims=True) + RMS_EPS) * g_ref[...] * (1.0 - LAMBDA_INIT)
    o_ref[...] = o.astype(o_ref.dtype)


def _attention(lam, qm, kr, vb, norm_g):
    h, _, s, _ = qm.shape
    t = ATTN_T
    return pl.pallas_call(
        _attn_kernel,
        grid=(h, s // t),
        in_specs=[
            pl.BlockSpec(memory_space=pltpu.SMEM),
            pl.BlockSpec((1, 2, t, LANES), lambda hh, i: (hh, 0, i, 0)),
            pl.BlockSpec((1, s, LANES), lambda hh, i: (hh, 0, 0)),
            pl.BlockSpec((1, s, LANES), lambda hh, i: (hh, 0, 0)),
            pl.BlockSpec((1, LANES), lambda hh, i: (0, 0)),
        ],
        out_specs=pl.BlockSpec((t, LANES), lambda hh, i: (i, hh)),
        out_shape=jax.ShapeDtypeStruct((s, h * LANES), BF16),
        scratch_shapes=[
            pltpu.VMEM((2 * t, LANES), F32),
            pltpu.VMEM((2 * t, LANES), F32),
            pltpu.VMEM((2 * t, LANES), F32),
        ],
        compiler_params=_cparams(("arbitrary", "arbitrary")),
        name="diff_attn",
    )(lam, qm, kr, vb, norm_g)


def _hgrn_kernel(q_ref, f_ref, i_ref, g_ref, lb_ref, ng_ref, o_ref,
                 st_ref, qs_ref, kk_ref, b_ref, kh_ref, od_ref, iv_ref):
    tr, sub, grp = HG_TR, HG_SUB, HG_GRP

    @pl.when(pl.program_id(1) == 0)
    def _():
        st_ref[...] = jnp.zeros(st_ref.shape, F32)

    lb = lb_ref[...]
    f = lb + (1.0 - lb) * jax.nn.sigmoid(f_ref[...])
    log_f = jnp.log(f)
    kk = 1.0 - f
    qs_ref[...] = _silu(q_ref[...])
    zpad = jnp.zeros((sub, LANES), F32)
    kk_ref[0:sub, :] = zpad
    b_ref[0:sub, :] = zpad
    iv_ref[0:sub, :] = zpad
    kk_ref[sub:, :] = kk
    iv_ref[sub:, :] = i_ref[...]

    r = lax.broadcasted_iota(jnp.int32, (grp, grp), 0)
    c = lax.broadcasted_iota(jnp.int32, (grp, grp), 1)
    same = (r // sub) == (c // sub)
    tri = jnp.where(same & (c <= r), 1.0, 0.0).astype(F32)
    blk = jnp.where(same, 1.0, 0.0).astype(F32)
    grp_iota = lax.broadcasted_iota(jnp.int32, (grp, LANES), 0)
    pos_in_sub = grp_iota % sub

    for gi in range(tr // grp):
        g0 = gi * grp
        rows = slice(g0, g0 + grp)
        prow = slice(sub + g0, sub + g0 + grp)
        lf = log_f[rows]
        bg = jnp.dot(tri, lf, preferred_element_type=F32, precision=lax.Precision.HIGHEST)
        tot = jnp.dot(blk, lf, preferred_element_type=F32, precision=lax.Precision.HIGHEST)
        b_ref[prow, :] = bg
        kh_ref[rows, :] = kk[rows] * jnp.exp(tot - bg)
        qg = qs_ref[rows, :]
        acc = jnp.sum(qg * kk[rows], axis=-1, keepdims=True) * i_ref[rows, :]
        for lag in range(1, sub):
            srow = slice(sub + g0 - lag, sub + g0 - lag + grp)
            e = jnp.exp(jnp.where(pos_in_sub >= lag, bg - b_ref[srow, :], NEG))
            w = jnp.sum(qg * kk_ref[srow, :] * e, axis=-1, keepdims=True)
            acc = acc + w * iv_ref[srow, :]
        od_ref[rows, :] = acc

    st = st_ref[...]
    for gi in range(tr // grp):
        g0 = gi * grp
        iv_t = i_ref[g0:g0 + grp, :].T.astype(BF16)
        kh_g = kh_ref[g0:g0 + grp, :]
        upds = []
        for j in range(grp // sub):
            in_sub = (grp_iota >= j * sub) & (grp_iota < (j + 1) * sub)
            upds.append(jnp.dot(iv_t, jnp.where(in_sub, kh_g, 0.0).astype(BF16), preferred_element_type=F32))
        for j in range(grp // sub):
            base = g0 + j * sub
            bj = b_ref[sub + base:sub + base + sub, :]
            qj = qs_ref[base:base + sub, :]
            o_inter = lax.dot_general((qj * jnp.exp(bj)).astype(BF16), st.astype(BF16),
                                      (((1,), (1,)), ((), ())), preferred_element_type=F32)
            od_ref[base:base + sub, :] = od_ref[base:base + sub, :] + o_inter
            st = st * jnp.exp(bj[sub - 1:sub]) + upds[j]
    st_ref[...] = st

    o = od_ref[...]
    o = o * lax.rsqrt(jnp.mean(o * o, axis=-1, keepdims=True) + RMS_EPS) * ng_ref[...]
    o_ref[...] = (o * _silu(g_ref[...])).astype(o_ref.dtype)


def _hgrn(proj, lower_bound, norm_g, col0):
    s = proj.shape[0]
    tr, h = HG_TR, HG_HEADS
    blk = lambda off: pl.BlockSpec((tr, LANES), lambda hh, i: (i, col0 + off + hh))
    pad = pltpu.VMEM((tr + HG_SUB, LANES), F32)
    full = pltpu.VMEM((tr, LANES), F32)
    return pl.pallas_call(
        _hgrn_kernel,
        grid=(h, s // tr),
        in_specs=[blk(0), blk(h), blk(2 * h), blk(3 * h),
                  pl.BlockSpec((1, LANES), lambda hh, i: (0, hh)),
                  pl.BlockSpec((1, LANES), lambda hh, i: (0, 0))],
        out_specs=pl.BlockSpec((tr, LANES), lambda hh, i: (i, hh)),
        out_shape=jax.ShapeDtypeStruct((s, h * LANES), BF16),
        scratch_shapes=[pltpu.VMEM((HG_DIM, HG_DIM), F32), full, pad, pad, full, full, pad],
        compiler_params=_cparams(("arbitrary", "arbitrary")),
        name="hgrn2",
    )(proj, proj, proj, proj, lower_bound, norm_g)


def _outproj_kernel(a_ref, w_ref, x_ref, gi_ref, bi_ref, g1_ref, b1_ref, h_ref, acc_ref):
    k = pl.program_id(1)

    @pl.when(k == 0)
    def _():
        acc_ref[...] = jnp.zeros(acc_ref.shape, F32)

    acc_ref[...] += jnp.dot(a_ref[...], w_ref[...].astype(BF16), preferred_element_type=F32)

    @pl.when(k == pl.num_programs(1) - 1)
    def _():
        h0 = _layer_norm(x_ref[...], gi_ref[...], bi_ref[...])
        h_ref[...] = _layer_norm(ALPHA * h0 + acc_ref[...], g1_ref[...], b1_ref[...])


def _outproj(a, w, x, gi, bi, g1, b1):
    s, d = x.shape
    kdim = a.shape[1]
    tm, tk = OUT_TM, OUT_TK
    vec = pl.BlockSpec((1, d), lambda i, k: (0, 0))
    return pl.pallas_call(
        _outproj_kernel,
        grid=(s // tm, kdim // tk),
        in_specs=[
            pl.BlockSpec((tm, tk), lambda i, k: (i, k)),
            pl.BlockSpec((tk, d), lambda i, k: (k, 0)),
            pl.BlockSpec((tm, d), lambda i, k: (i, 0)),
            vec, vec, vec, vec,
        ],
        out_specs=pl.BlockSpec((tm, d), lambda i, k: (i, 0)),
        out_shape=jax.ShapeDtypeStruct((s, d), F32),
        scratch_shapes=[pltpu.VMEM((tm, d), F32)],
        compiler_params=_cparams(("arbitrary", "arbitrary")),
        name="outproj_ln1",
    )(a, w, x, gi, bi, g1, b1)


def _router_kernel(h_ref, w_ref, eb_ref, idx_ref, wt_ref, rank_ref, cnt_ref, carry_ref):
    tm = ROUTER_TM
    i = pl.program_id(0)

    @pl.when(i == 0)
    def _():
        carry_ref[...] = jnp.zeros(carry_ref.shape, F32)

    logits = jnp.dot(h_ref[...], w_ref[...], preferred_element_type=F32, precision=lax.Precision.HIGHEST)
    scores = jax.nn.sigmoid(logits)
    choice = scores + eb_ref[...]
    lane = lax.broadcasted_iota(jnp.int32, choice.shape, 1)
    grp_of_lane = lane // GROUP_SIZE
    big = jnp.int32(N_EXPERTS)
    ninf = -jnp.inf

    def first_argmax(vals):
        m = jnp.max(vals, axis=-1, keepdims=True)
        idx = jnp.min(jnp.where(vals == m, lane, big), axis=-1, keepdims=True)
        return m, idx

    gs = []
    for g in range(N_GROUPS):
        vals = jnp.where(grp_of_lane == g, choice, ninf)
        m1, i1 = first_argmax(vals)
        m2 = jnp.max(jnp.where(lane == i1, ninf, vals), axis=-1, keepdims=True)
        gs.append(m1 + m2)
    allowed = jnp.zeros(choice.shape, jnp.int32)
    for g in range(N_GROUPS):
        beats = jnp.zeros(gs[g].shape, jnp.int32)
        for g2 in range(N_GROUPS):
            if g2 == g:
                continue
            better = (gs[g2] >= gs[g]) if g2 < g else (gs[g2] > gs[g])
            beats = beats + jnp.where(better, 1, 0)
        allowed = jnp.where(grp_of_lane == g, jnp.where(beats < TOPK_GROUPS, 1, 0), allowed)
    masked = jnp.where(allowed > 0, choice, ninf)

    sel = jnp.zeros(choice.shape, F32)
    idxs, wts = [], []
    for _ in range(TOP_K):
        _, ik = first_argmax(masked)
        hit = lane == ik
        idxs.append(ik)
        wts.append(jnp.sum(jnp.where(hit, scores, 0.0), axis=-1, keepdims=True))
        sel = jnp.where(hit, 1.0, sel)
        masked = jnp.where(hit, ninf, masked)
    wsum = wts[0]
    for k in range(1, TOP_K):
        wsum = wsum + wts[k]

    r = lax.broadcasted_iota(jnp.int32, (tm, tm), 0)
    c = lax.broadcasted_iota(jnp.int32, (tm, tm), 1)
    strict_lower = jnp.where(c < r, 1.0, 0.0).astype(BF16)
    rank = jnp.dot(strict_lower, sel.astype(BF16), preferred_element_type=F32) + carry_ref[...]
    carry_ref[...] = carry_ref[...] + jnp.sum(sel, axis=0, keepdims=True)
    cnt_ref[...] = carry_ref[...]

    out_lane = lax.broadcasted_iota(jnp.int32, (tm, LANES), 1)
    idx_out = jnp.zeros((tm, LANES), jnp.int32)
    wt_out = jnp.zeros((tm, LANES), F32)
    rank_out = jnp.zeros((tm, LANES), jnp.int32)
    for k in range(TOP_K):
        rk = jnp.sum(jnp.where(lane == idxs[k], rank, 0.0), axis=-1, keepdims=True)
        idx_out = jnp.where(out_lane == k, idxs[k], idx_out)
        wt_out = jnp.where(out_lane == k, wts[k] / wsum * ROUTED_SCALE, wt_out)
        rank_out = jnp.where(out_lane == k, rk.astype(jnp.int32), rank_out)
    idx_ref[...] = idx_out
    wt_ref[...] = wt_out
    rank_ref[...] = rank_out


def _router(h, w_router, e_bias):
    s, d = h.shape
    e = w_router.shape[1]
    tm = ROUTER_TM
    row = pl.BlockSpec((tm, LANES), lambda i: (i, 0))
    return pl.pallas_call(
        _router_kernel,
        grid=(s // tm,),
        in_specs=[
            pl.BlockSpec((tm, d), lambda i: (i, 0)),
            pl.BlockSpec((d, e), lambda i: (0, 0)),
            pl.BlockSpec((1, e), lambda i: (0, 0)),
        ],
        out_specs=[row, row, row, pl.BlockSpec((1, e), lambda i: (0, 0))],
        out_shape=[
            jax.ShapeDtypeStruct((s, LANES), jnp.int32),
            jax.ShapeDtypeStruct((s, LANES), F32),
            jax.ShapeDtypeStruct((s, LANES), jnp.int32),
            jax.ShapeDtypeStruct((1, e), F32),
        ],
        scratch_shapes=[pltpu.VMEM((1, e), F32)],
        compiler_params=_cparams(("arbitrary",)),
        name="router_topk",
    )(h, w_router, e_bias)


def _dest_kernel(idx_ref, rank_ref, ps_ref, dest_ref):
    idx = idx_ref[...]
    rank = rank_ref[...]
    ps = ps_ref[...]
    lane = lax.broadcasted_iota(jnp.int32, (idx.shape[0], ps.shape[1]), 1)
    out_lane = lax.broadcasted_iota(jnp.int32, idx.shape, 1)
    dest = jnp.zeros(idx.shape, jnp.int32)
    for k in range(TOP_K):
        start_k = jnp.sum(jnp.where(lane == idx[:, k:k + 1], ps, 0.0), axis=-1, keepdims=True)
        dest = jnp.where(out_lane == k, start_k.astype(jnp.int32) + rank, dest)
    dest_ref[...] = dest.T[:TOP_K, :]


def _dest(idx, rank, pstarts):
    s = idx.shape[0]
    e = pstarts.shape[1]
    tm = ROUTER_TM
    row = pl.BlockSpec((tm, LANES), lambda i: (i, 0))
    return pl.pallas_call(
        _dest_kernel,
        grid=(s // tm,),
        in_specs=[row, row, pl.BlockSpec((1, e), lambda i: (0, 0))],
        out_specs=pl.BlockSpec((TOP_K, tm), lambda i: (0, i)),
        out_shape=jax.ShapeDtypeStruct((TOP_K, s), jnp.int32),
        compiler_params=_cparams(("arbitrary",)),
        name="moe_dest",
    )(idx, rank, pstarts)


def _dispatch_kernel(dest_ref, h_ref, xs_ref, sem):
    tt = DISPATCH_TT

    def row_copy(t, k):
        return pltpu.make_async_copy(h_ref.at[pl.ds(t, 1)], xs_ref.at[pl.ds(dest_ref[k, t], 1)], sem)

    def issue(t, carry):
        for k in range(TOP_K):
            row_copy(t, k).start()
        return carry

    def drain(t, carry):
        for k in range(TOP_K):
            row_copy(t, k).wait()
        return carry

    lax.fori_loop(0, tt, issue, 0)
    lax.fori_loop(0, tt, drain, 0)


def _dispatch(dest_t, h, n_rows):
    s, d = h.shape
    tt = DISPATCH_TT
    return pl.pallas_call(
        _dispatch_kernel,
        grid=(s // tt,),
        in_specs=[
            pl.BlockSpec((TOP_K, tt), lambda i: (0, i), memory_space=pltpu.SMEM),
            pl.BlockSpec((tt, d), lambda i: (i, 0)),
        ],
        out_specs=pl.BlockSpec(memory_space=pl.ANY),
        out_shape=jax.ShapeDtypeStruct((n_rows, d), h.dtype),
        scratch_shapes=[pltpu.SemaphoreType.DMA(())],
        compiler_params=_cparams(("arbitrary",)),
        name="moe_dispatch",
    )(dest_t, h)


def _expert_kernel(start_ref, cnt_ref, xs_ref, wg_ref, wu_ref, wd_ref, y_ref,
                   wgb_ref, wub_ref, wdb_ref, xin_ref, yout_ref, in_sem, out_sem, st_ref):
    e = pl.program_id(0)
    ne = pl.num_programs(0)
    rb = MOE_RB

    def in_copy(row0, slot):
        return pltpu.make_async_copy(xs_ref.at[pl.ds(row0, rb)], xin_ref.at[slot], in_sem.at[slot])

    def out_copy(row0, slot):
        return pltpu.make_async_copy(yout_ref.at[slot], y_ref.at[pl.ds(row0, rb)], out_sem.at[slot])

    @pl.when(e == 0)
    def _():
        st_ref[0] = 0
        st_ref[1] = 0
        st_ref[2] = 0
        in_copy(pl.multiple_of(start_ref[0], ROW_ALIGN), 0).start()

    wgb_ref[...] = wg_ref[0].astype(BF16)
    wub_ref[...] = wu_ref[0].astype(BF16)
    wdb_ref[...] = wd_ref[0].astype(BF16)

    start = start_ref[e]
    cnt = cnt_ref[e]
    n_sub = jnp.maximum(1, (cnt + rb - 1) // rb)
    next_start = start_ref[jnp.minimum(e + 1, ne - 1)]

    def sub_block(j, carry):
        slot = st_ref[0]
        done = st_ref[1]
        row0 = pl.multiple_of(start + j * rb, ROW_ALIGN)
        last_j = j == n_sub - 1
        nxt = pl.multiple_of(jnp.where(last_j, next_start, row0 + rb), ROW_ALIGN)

        @pl.when(jnp.logical_not(last_j & (e == ne - 1)))
        def _():
            in_copy(nxt, 1 - slot).start()

        in_copy(row0, slot).wait()
        x = xin_ref[slot]
        live = (lax.broadcasted_iota(jnp.int32, x.shape, 0) + j * rb) < cnt
        xb = jnp.where(live, x, 0.0).astype(BF16)
        gate = jnp.dot(xb, wgb_ref[...], preferred_element_type=F32)
        up = jnp.dot(xb, wub_ref[...], preferred_element_type=F32)
        hdn = (_silu(gate) * up).astype(BF16)
        yout_ref[slot] = jnp.dot(hdn, wdb_ref[...], preferred_element_type=F32)

        @pl.when(done > 0)
        def _():
            out_copy(pl.multiple_of(st_ref[2], ROW_ALIGN), 1 - slot).wait()

        out_copy(row0, slot).start()
        st_ref[0] = 1 - slot
        st_ref[1] = done + 1
        st_ref[2] = row0
        return carry

    lax.fori_loop(0, n_sub, sub_block, 0)

    @pl.when(e == ne - 1)
    def _():
        out_copy(pl.multiple_of(st_ref[2], ROW_ALIGN), 1 - st_ref[0]).wait()


def _experts(starts, counts, xs, w_gate, w_up, w_down):
    r, d = xs.shape
    ne, _, f = w_gate.shape
    rb = MOE_RB
    wsel = lambda e, st, ct: (e, 0, 0)
    grid_spec = pltpu.PrefetchScalarGridSpec(
        num_scalar_prefetch=2,
        grid=(ne,),
        in_specs=[
            pl.BlockSpec(memory_space=pl.ANY),
            pl.BlockSpec((1, d, f), wsel),
            pl.BlockSpec((1, d, f), wsel),
            pl.BlockSpec((1, f, d), wsel),
        ],
        out_specs=pl.BlockSpec(memory_space=pl.ANY),
        scratch_shapes=[
            pltpu.VMEM((d, f), BF16), pltpu.VMEM((d, f), BF16), pltpu.VMEM((f, d), BF16),
            pltpu.VMEM((2, rb, d), F32), pltpu.VMEM((2, rb, d), F32),
            pltpu.SemaphoreType.DMA((2,)), pltpu.SemaphoreType.DMA((2,)),
            pltpu.SMEM((3,), jnp.int32),
        ],
    )
    return pl.pallas_call(
        _expert_kernel,
        grid_spec=grid_spec,
        out_shape=jax.ShapeDtypeStruct((r, d), F32),
        compiler_params=_cparams(("arbitrary",)),
        name="moe_experts",
    )(starts, counts, xs, w_gate, w_up, w_down)


def _shared_kernel(h_ref, wg_ref, wu_ref, wd_ref, o_ref, wgb_ref, wub_ref, wdb_ref):
    @pl.when(pl.program_id(0) == 0)
    def _():
        wgb_ref[...] = wg_ref[...].astype(BF16)
        wub_ref[...] = wu_ref[...].astype(BF16)
        wdb_ref[...] = wd_ref[...].astype(BF16)

    hb = h_ref[...].astype(BF16)
    gate = jnp.dot(hb, wgb_ref[...], preferred_element_type=F32)
    up = jnp.dot(hb, wub_ref[...], preferred_element_type=F32)
    hdn = (_silu(gate) * up).astype(BF16)
    o_ref[...] = jnp.dot(hdn, wdb_ref[...], preferred_element_type=F32)


def _shared(h, ws_gate, ws_up, ws_down):
    s, d = h.shape
    f = ws_gate.shape[1]
    tm = SHARED_TM
    return pl.pallas_call(
        _shared_kernel,
        grid=(s // tm,),
        in_specs=[
            pl.BlockSpec((tm, d), lambda i: (i, 0)),
            pl.BlockSpec((d, f), lambda i: (0, 0)),
            pl.BlockSpec((d, f), lambda i: (0, 0)),
            pl.BlockSpec((f, d), lambda i: (0, 0)),
        ],
        out_specs=pl.BlockSpec((tm, d), lambda i: (i, 0)),
        out_shape=jax.ShapeDtypeStruct((s, d), F32),
        scratch_shapes=[pltpu.VMEM((d, f), BF16), pltpu.VMEM((d, f), BF16), pltpu.VMEM((f, d), BF16)],
        compiler_params=_cparams(("arbitrary",)),
        name="moe_shared",
    )(h, ws_gate, ws_up, ws_down)


def _combine_kernel(dest_ref, y_ref, wt_ref, h_ref, sh_ref, g_ref, b_ref, o_ref, rows_ref, sem):
    tt = COMBINE_TT

    def row_copy(t, k):
        return pltpu.make_async_copy(y_ref.at[pl.ds(dest_ref[k, t], 1)], rows_ref.at[k, pl.ds(t, 1)], sem)

    def issue(t, carry):
        for k in range(TOP_K):
            row_copy(t, k).start()
        return carry

    def drain(t, carry):
        for k in range(TOP_K):
            row_copy(t, k).wait()
        return carry

    lax.fori_loop(0, tt, issue, 0)
    lax.fori_loop(0, tt, drain, 0)

    wt = wt_ref[...]
    routed = rows_ref[0] * wt[:, 0:1]
    for k in range(1, TOP_K):
        routed = routed + rows_ref[k] * wt[:, k:k + 1]
    ff = routed + sh_ref[...]
    o_ref[...] = _layer_norm(ALPHA * h_ref[...] + ff, g_ref[...], b_ref[...])


def _combine(dest_t, y, wt, h, shared, g, b):
    s, d = h.shape
    tt = COMBINE_TT
    row = pl.BlockSpec((tt, d), lambda i: (i, 0))
    vec = pl.BlockSpec((1, d), lambda i: (0, 0))
    return pl.pallas_call(
        _combine_kernel,
        grid=(s // tt,),
        in_specs=[
            pl.BlockSpec((TOP_K, tt), lambda i: (0, i), memory_space=pltpu.SMEM),
            pl.BlockSpec(memory_space=pl.ANY),
            pl.BlockSpec((tt, LANES), lambda i: (i, 0)),
            row, row, vec, vec,
        ],
        out_specs=row,
        out_shape=jax.ShapeDtypeStruct((s, d), F32),
        scratch_shapes=[pltpu.VMEM((TOP_K, tt, d), F32), pltpu.SemaphoreType.DMA(())],
        compiler_params=_cparams(("arbitrary",)),
        name="moe_combine_ln2",
    )(dest_t, y, wt, h, shared, g, b)


def _rope_tables(positions):
    half = DA_QK_DIM // 2
    inv = ROPE_THETA ** (-jnp.arange(0, DA_QK_DIM, 2, dtype=F32) / DA_QK_DIM)
    ang = positions.astype(F32)[:, None] * inv
    cos = jnp.tile(jnp.cos(ang), (1, LANES // half))
    sin = jnp.sin(ang)
    sin = jnp.tile(jnp.concatenate([-sin, sin], axis=-1), (1, LANES // DA_QK_DIM))
    return cos, sin


def _moe(h, w_router, e_bias, w_gate, w_up, w_down, ws_gate, ws_up, ws_down, ln_g, ln_b):
    s, d = h.shape
    e = w_router.shape[1]
    idx, wt, rank, counts = _router(h, w_router, e_bias.reshape(1, e))
    counts = counts[0].astype(jnp.int32)
    acounts = (counts + ROW_ALIGN - 1) // ROW_ALIGN * ROW_ALIGN
    ends = jnp.cumsum(acounts)
    starts = (ends - acounts).astype(jnp.int32)
    dest_t = _dest(idx, rank, starts.astype(F32).reshape(1, e))
    n_rows = s * TOP_K + e * ROW_ALIGN + MOE_RB

    xs = _dispatch(dest_t, h, n_rows)
    y = _experts(starts, counts, xs, w_gate, w_up, w_down)
    shared = _shared(h, ws_gate, ws_up, ws_down)
    return _combine(dest_t, y, wt, h, shared, ln_g.reshape(1, d), ln_b.reshape(1, d))


def kernel(x, positions, ln_in_g, ln_in_b, w_in, w_out, lam_q1, lam_k1, lam_q2, lam_k2, da_norm_g,
           hg_lb_logits, hg_norm_g, ln1_g, ln1_b, w_router, e_bias, w_gate, w_up, w_down,
           ws_gate, ws_up, ws_down, ln2_g, ln2_b):
    bsz, s, d = x.shape
    assert bsz == 1 and w_in.shape[0] == DEPTH
    x2 = x.reshape(s, d)
    cos, sin = _rope_tables(positions.reshape(s))
    lower_bounds = jnp.cumsum(jax.nn.softmax(hg_lb_logits.astype(F32), axis=0), axis=0)
    lam = (jnp.exp(jnp.sum(lam_q1[0] * lam_k1[0])) - jnp.exp(jnp.sum(lam_q2[0] * lam_k2[0]))
           + LAMBDA_INIT).reshape(1).astype(F32)
    gi, bi = ln_in_g.reshape(1, d), ln_in_b.reshape(1, d)

    proj = _inproj(x2, gi, bi, w_in[0])
    qm, kr, vb = _rope(proj, cos, sin)
    y_a = _attention(lam, qm, kr, vb, da_norm_g[0].reshape(1, LANES))
    hg_col0 = 3 * DA_HEADS
    y_b = _hgrn(proj, lower_bounds[0].reshape(1, -1), hg_norm_g[0].reshape(1, LANES), hg_col0)
    mix_in = jnp.concatenate([y_a, y_b], axis=-1)
    h1 = _outproj(mix_in, w_out[0], x2, gi, bi, ln1_g[0].reshape(1, d), ln1_b[0].reshape(1, d))
    h2 = _moe(h1, w_router[0], e_bias[0], w_gate[0], w_up[0], w_down[0],
              ws_gate[0], ws_up[0], ws_down[0], ln2_g[0], ln2_b[0])
    return h2.reshape(bsz, s, d)
```

```python
import math

import jax
import jax.numpy as jnp
import numpy as np
from jax import lax
from jax.experimental import pallas as pl
from jax.experimental.pallas import tpu as pltpu

F32 = jnp.float32
BF16 = jnp.bfloat16
U32 = jnp.uint32
HI_MASK = np.uint32(0xFFFF0000)

CHUNK = 64
DA_HEADS = 8
DA_QK_DIM = 64
DA_V_DIM = 2 * DA_QK_DIM
HG_HEADS = 8
HG_DIM = 128
ROPE_THETA = 10000.0
N_EXPERTS = 256
TOP_K = 8
N_GROUPS = 8
TOPK_GROUPS = 4
GROUP_SIZE = N_EXPERTS // N_GROUPS
ROUTED_SCALE = 2.5
DEPTH = 1
ALPHA = (2.0 * DEPTH) ** 0.25
LN_EPS = 1e-5
RMS_EPS = 1e-5
LAMBDA_INIT = 0.8 - 0.6 * math.exp(-0.3 * 0)

LANES = 128
VMEM_LIMIT = 56 * 1024 * 1024
NEG = -1e30

INPROJ_TM, INPROJ_TN = 1024, 512
ROPE_TR = 512
ATTN_T = 1024
HG_TR = 512
HG_SUB = 16
HG_GRP = 128
OUT_TM, OUT_TK = 512, 512
ROUTER_TM = 512
MOE_RB = 256
MOE_CH = 64
ROW_ALIGN = 8
DISPATCH_TT = 256
COMBINE_TT = 128


def _cparams(sem):
    return pltpu.CompilerParams(dimension_semantics=sem, vmem_limit_bytes=VMEM_LIMIT)


def _layer_norm(xf, g, b):
    mu = jnp.mean(xf, axis=-1, keepdims=True)
    xc = xf - mu
    var = jnp.mean(xc * xc, axis=-1, keepdims=True)
    return xc * lax.rsqrt(var + LN_EPS) * g + b


def _silu(x):
    return x * jax.nn.sigmoid(x)


def _inproj_kernel(x_ref, g_ref, b_ref, w_ref, o_ref, hb_ref):
    @pl.when(pl.program_id(1) == 0)
    def _():
        hb_ref[...] = _layer_norm(x_ref[...], g_ref[...], b_ref[...]).astype(BF16)

    o_ref[...] = jnp.dot(hb_ref[...], w_ref[...].astype(BF16), preferred_element_type=F32)


def _inproj(x, g, b, w):
    s, d = x.shape
    n = w.shape[1]
    tm, tn = INPROJ_TM, INPROJ_TN
    return pl.pallas_call(
        _inproj_kernel,
        grid=(s // tm, n // tn),
        in_specs=[
            pl.BlockSpec((tm, d), lambda i, j: (i, 0)),
            pl.BlockSpec((1, d), lambda i, j: (0, 0)),
            pl.BlockSpec((1, d), lambda i, j: (0, 0)),
            pl.BlockSpec((d, tn), lambda i, j: (0, j)),
        ],
        out_specs=pl.BlockSpec((tm, tn), lambda i, j: (i, j)),
        out_shape=jax.ShapeDtypeStruct((s, n), F32),
        scratch_shapes=[pltpu.VMEM((tm, d), BF16)],
        compiler_params=_cparams(("arbitrary", "arbitrary")),
        name="inproj",
    )(x, g, b, w)


def _rope_kernel(q_ref, k_ref, v_ref, cos_ref, sin_ref, qm_ref, kr_ref, vb_ref):
    cos = cos_ref[...]
    sin = sin_ref[...]
    lane = lax.broadcasted_iota(jnp.int32, cos.shape, 1)
    first_half = (lane % DA_QK_DIM) < (DA_QK_DIM // 2)

    def rope(x):
        rot = jnp.where(first_half, pltpu.roll(x, LANES - DA_QK_DIM // 2, 1), pltpu.roll(x, DA_QK_DIM // 2, 1))
        return x * cos + rot * sin

    q = rope(q_ref[...]) * (DA_QK_DIM ** -0.5)
    k = rope(k_ref[...])
    map0 = lane < DA_QK_DIM
    qm_ref[0, 0] = jnp.where(map0, q, 0.0).astype(BF16)
    qm_ref[0, 1] = jnp.where(map0, 0.0, q).astype(BF16)
    kr_ref[0] = k.astype(BF16)
    vb_ref[0] = v_ref[...].astype(BF16)


def _rope(proj, cos, sin):
    s = proj.shape[0]
    tr = ROPE_TR
    h = DA_HEADS
    blk = lambda off: pl.BlockSpec((tr, LANES), lambda i, hh: (i, off + hh))
    tab = pl.BlockSpec((tr, LANES), lambda i, hh: (i, 0))
    return pl.pallas_call(
        _rope_kernel,
        grid=(s // tr, h),
        in_specs=[blk(0), blk(h), blk(2 * h), tab, tab],
        out_specs=[
            pl.BlockSpec((1, 2, tr, LANES), lambda i, hh: (hh, 0, i, 0)),
            pl.BlockSpec((1, tr, LANES), lambda i, hh: (hh, i, 0)),
            pl.BlockSpec((1, tr, LANES), lambda i, hh: (hh, i, 0)),
        ],
        out_shape=[
            jax.ShapeDtypeStruct((h, 2, s, LANES), BF16),
            jax.ShapeDtypeStruct((h, s, LANES), BF16),
            jax.ShapeDtypeStruct((h, s, LANES), BF16),
        ],
        compiler_params=_cparams(("arbitrary", "arbitrary")),
        name="rope",
    )(proj, proj, proj, cos, sin)


def _attn_kernel(lam_ref, q_ref, k_ref, v_ref, g_ref, o_ref, acc_ref, m_ref, l_ref):
    t = ATTN_T
    i = pl.program_id(1)
    q = q_ref[0].reshape(2 * t, LANES)
    m_ref[...] = jnp.full(m_ref.shape, NEG, F32)
    l_ref[...] = jnp.zeros(l_ref.shape, F32)
    acc_ref[...] = jnp.zeros(acc_ref.shape, F32)

    def step(c, masked):
        start = pl.multiple_of(c * t, t)
        k = k_ref[0, pl.ds(start, t), :]
        v = v_ref[0, pl.ds(start, t), :]
        s = lax.dot_general(q, k, (((1,), (1,)), ((), ())), preferred_element_type=F32)
        if masked:
            row = lax.broadcasted_iota(jnp.int32, s.shape, 0) % t
            col = lax.broadcasted_iota(jnp.int32, s.shape, 1)
            s = jnp.where((col // CHUNK) <= (row // CHUNK), s, NEG)
        cols = [s[:, j * LANES:(j + 1) * LANES] for j in range(t // LANES)]
        mx = cols[0]
        for cj in cols[1:]:
            mx = jnp.maximum(mx, cj)
        m_prev = m_ref[...]
        m_new = jnp.maximum(m_prev, jnp.max(mx, axis=-1, keepdims=True))
        alpha = jnp.exp(m_prev - m_new)
        ps = [jnp.exp(cj - m_new) for cj in cols]
        lsum = ps[0]
        for pj in ps[1:]:
            lsum = lsum + pj
        l_ref[...] = alpha * l_ref[...] + lsum
        m_ref[...] = m_new
        p = jnp.concatenate([pj.astype(BF16) for pj in ps], axis=-1)
        acc_ref[...] = alpha * acc_ref[...] + jnp.dot(p, v, preferred_element_type=F32)

    def body(c, carry):
        step(c, False)
        return carry

    lax.fori_loop(0, i, body, 0)
    step(i, True)

    lam = lam_ref[0]
    acc = acc_ref[...]
    l = jnp.sum(l_ref[...], axis=-1, keepdims=True)
    o = acc[:t] / l[:t] - lam * (acc[t:] / l[t:])
    o = o * lax.rsqrt(jnp.mean(o * o, axis=-1, keepdims=True) + RMS_EPS) * g_ref[...] * (1.0 - LAMBDA_INIT)
    o_ref[...] = o.astype(o_ref.dtype)


def _attention(lam, qm, kr, vb, norm_g):
    h, _, s, _ = qm.shape
    t = ATTN_T
    return pl.pallas_call(
        _attn_kernel,
        grid=(h, s // t),
        in_specs=[
            pl.BlockSpec(memory_space=pltpu.SMEM),
            pl.BlockSpec((1, 2, t, LANES), lambda hh, i: (hh, 0, i, 0)),
            pl.BlockSpec((1, s, LANES), lambda hh, i: (hh, 0, 0)),
            pl.BlockSpec((1, s, LANES), lambda hh, i: (hh, 0, 0)),
            pl.BlockSpec((1, LANES), lambda hh, i: (0, 0)),
        ],
        out_specs=pl.BlockSpec((t, LANES), lambda hh, i: (i, hh)),
        out_shape=jax.ShapeDtypeStruct((s, h * LANES), BF16),
        scratch_shapes=[
            pltpu.VMEM((2 * t, LANES), F32),
            pltpu.VMEM((2 * t, LANES), F32),
            pltpu.VMEM((2 * t, LANES), F32),
        ],
        compiler_params=_cparams(("arbitrary", "arbitrary")),
        name="diff_attn",
    )(lam, qm, kr, vb, norm_g)


def _hgrn_kernel(q_ref, f_ref, i_ref, g_ref, lb_ref, ng_ref, o_ref,
                 st_ref, qs_ref, kk_ref, b_ref, kh_ref, od_ref, iv_ref):
    tr, sub, grp = HG_TR, HG_SUB, HG_GRP

    @pl.when(pl.program_id(1) == 0)
    def _():
        st_ref[...] = jnp.zeros(st_ref.shape, F32)

    lb = lb_ref[...]
    f = lb + (1.0 - lb) * jax.nn.sigmoid(f_ref[...])
    log_f = jnp.log(f)
    kk = 1.0 - f
    qs_ref[...] = _silu(q_ref[...])
    zpad = jnp.zeros((sub, LANES), F32)
    kk_ref[0:sub, :] = zpad
    b_ref[0:sub, :] = zpad
    iv_ref[0:sub, :] = zpad
    kk_ref[sub:, :] = kk
    iv_ref[sub:, :] = i_ref[...]

    r = lax.broadcasted_iota(jnp.int32, (grp, grp), 0)
    c = lax.broadcasted_iota(jnp.int32, (grp, grp), 1)
    same = (r // sub) == (c // sub)
    tri = jnp.where(same & (c <= r), 1.0, 0.0).astype(F32)
    blk = jnp.where(same, 1.0, 0.0).astype(F32)
    grp_iota = lax.broadcasted_iota(jnp.int32, (grp, LANES), 0)
    pos_in_sub = grp_iota % sub

    for gi in range(tr // grp):
        g0 = gi * grp
        rows = slice(g0, g0 + grp)
        prow = slice(sub + g0, sub + g0 + grp)
        lf = log_f[rows]
        bg = jnp.dot(tri, lf, preferred_element_type=F32, precision=lax.Precision.HIGHEST)
        tot = jnp.dot(blk, lf, preferred_element_type=F32, precision=lax.Precision.HIGHEST)
        b_ref[prow, :] = bg
        kh_ref[rows, :] = kk[rows] * jnp.exp(tot - bg)
        qg = qs_ref[rows, :]
        acc = jnp.sum(qg * kk[rows], axis=-1, keepdims=True) * i_ref[rows, :]
        for lag in range(1, sub):
            srow = slice(sub + g0 - lag, sub + g0 - lag + grp)
            e = jnp.exp(jnp.where(pos_in_sub >= lag, bg - b_ref[srow, :], NEG))
            w = jnp.sum(qg * kk_ref[srow, :] * e, axis=-1, keepdims=True)
            acc = acc + w * iv_ref[srow, :]
        od_ref[rows, :] = acc

    st = st_ref[...]
    for gi in range(tr // grp):
        g0 = gi * grp
        iv_t = i_ref[g0:g0 + grp, :].T.astype(BF16)
        kh_g = kh_ref[g0:g0 + grp, :]
        upds = []
        for j in range(grp // sub):
            in_sub = (grp_iota >= j * sub) & (grp_iota < (j + 1) * sub)
            upds.append(jnp.dot(iv_t, jnp.where(in_sub, kh_g, 0.0).astype(BF16), preferred_element_type=F32))
        for j in range(grp // sub):
            base = g0 + j * sub
            bj = b_ref[sub + base:sub + base + sub, :]
            qj = qs_ref[base:base + sub, :]
            o_inter = lax.dot_general((qj * jnp.exp(bj)).astype(BF16), st.astype(BF16),
                                      (((1,), (1,)), ((), ())), preferred_element_type=F32)
            od_ref[base:base + sub, :] = od_ref[base:base + sub, :] + o_inter
            st = st * jnp.exp(bj[sub - 1:sub]) + upds[j]
    st_ref[...] = st

    o = od_ref[...]
    o = o * lax.rsqrt(jnp.mean(o * o, axis=-1, keepdims=True) + RMS_EPS) * ng_ref[...]
    o_ref[...] = (o * _silu(g_ref[...])).astype(o_ref.dtype)


def _hgrn(proj, lower_bound, norm_g, col0):
    s = proj.shape[0]
    tr, h = HG_TR, HG_HEADS
    blk = lambda off: pl.BlockSpec((tr, LANES), lambda hh, i: (i, col0 + off + hh))
    pad = pltpu.VMEM((tr + HG_SUB, LANES), F32)
    full = pltpu.VMEM((tr, LANES), F32)
    return pl.pallas_call(
        _hgrn_kernel,
        grid=(h, s // tr),
        in_specs=[blk(0), blk(h), blk(2 * h), blk(3 * h),
                  pl.BlockSpec((1, LANES), lambda hh, i: (0, hh)),
                  pl.BlockSpec((1, LANES), lambda hh, i: (0, 0))],
        out_specs=pl.BlockSpec((tr, LANES), lambda hh, i: (i, hh)),
        out_shape=jax.ShapeDtypeStruct((s, h * LANES), BF16),
        scratch_shapes=[pltpu.VMEM((HG_DIM, HG_DIM), F32), full, pad, pad, full, full, pad],
        compiler_params=_cparams(("arbitrary", "arbitrary")),
        name="hgrn2",
    )(proj, proj, proj, proj, lower_bound, norm_g)


def _outproj_kernel(a_ref, w_ref, x_ref, gi_ref, bi_ref, g1_ref, b1_ref, h_ref, acc_ref):
    k = pl.program_id(1)

    @pl.when(k == 0)
    def _():
        acc_ref[...] = jnp.zeros(acc_ref.shape, F32)

    acc_ref[...] += jnp.dot(a_ref[...], w_ref[...].astype(BF16), preferred_element_type=F32)

    @pl.when(k == pl.num_programs(1) - 1)
    def _():
        h0 = _layer_norm(x_ref[...], gi_ref[...], bi_ref[...])
        h_ref[...] = _layer_norm(ALPHA * h0 + acc_ref[...], g1_ref[...], b1_ref[...])


def _outproj(a, w, x, gi, bi, g1, b1):
    s, d = x.shape
    kdim = a.shape[1]
    tm, tk = OUT_TM, OUT_TK
    vec = pl.BlockSpec((1, d), lambda i, k: (0, 0))
    return pl.pallas_call(
        _outproj_kernel,
        grid=(s // tm, kdim // tk),
        in_specs=[
            pl.BlockSpec((tm, tk), lambda i, k: (i, k)),
            pl.BlockSpec((tk, d), lambda i, k: (k, 0)),
            pl.BlockSpec((tm, d), lambda i, k: (i, 0)),
            vec, vec, vec, vec,
        ],
        out_specs=pl.BlockSpec((tm, d), lambda i, k: (i, 0)),
        out_shape=jax.ShapeDtypeStruct((s, d), F32),
        scratch_shapes=[pltpu.VMEM((tm, d), F32)],
        compiler_params=_cparams(("arbitrary", "arbitrary")),
        name="outproj_ln1",
    )(a, w, x, gi, bi, g1, b1)


def _router_kernel(h_ref, w_ref, eb_ref, idx_ref, wt_ref, rank_ref, cnt_ref, carry_ref):
    tm = ROUTER_TM
    i = pl.program_id(0)

    @pl.when(i == 0)
    def _():
        carry_ref[...] = jnp.zeros(carry_ref.shape, F32)

    logits = jnp.dot(h_ref[...], w_ref[...], preferred_element_type=F32, precision=lax.Precision.HIGHEST)
    scores = jax.nn.sigmoid(logits)
    choice = scores + eb_ref[...]
    lane = lax.broadcasted_iota(jnp.int32, choice.shape, 1)
    grp_of_lane = lane // GROUP_SIZE
    big = jnp.int32(N_EXPERTS)
    ninf = -jnp.inf

    def first_argmax(vals):
        m = jnp.max(vals, axis=-1, keepdims=True)
        idx = jnp.min(jnp.where(vals == m, lane, big), axis=-1, keepdims=True)
        return m, idx

    gs = []
    for g in range(N_GROUPS):
        vals = jnp.where(grp_of_lane == g, choice, ninf)
        m1, i1 = first_argmax(vals)
        m2 = jnp.max(jnp.where(lane == i1, ninf, vals), axis=-1, keepdims=True)
        gs.append(m1 + m2)
    allowed = jnp.zeros(choice.shape, jnp.int32)
    for g in range(N_GROUPS):
        beats = jnp.zeros(gs[g].shape, jnp.int32)
        for g2 in range(N_GROUPS):
            if g2 == g:
                continue
            better = (gs[g2] >= gs[g]) if g2 < g else (gs[g2] > gs[g])
            beats = beats + jnp.where(better, 1, 0)
        allowed = jnp.where(grp_of_lane == g, jnp.where(beats < TOPK_GROUPS, 1, 0), allowed)
    masked = jnp.where(allowed > 0, choice, ninf)

    sel = jnp.zeros(choice.shape, F32)
    idxs, wts = [], []
    for _ in range(TOP_K):
        _, ik = first_argmax(masked)
        hit = lane == ik
        idxs.append(ik)
        wts.append(jnp.sum(jnp.where(hit, scores, 0.0), axis=-1, keepdims=True))
        sel = jnp.where(hit, 1.0, sel)
        masked = jnp.where(hit, ninf, masked)
    wsum = wts[0]
    for k in range(1, TOP_K):
        wsum = wsum + wts[k]

    r = lax.broadcasted_iota(jnp.int32, (tm, tm), 0)
    c = lax.broadcasted_iota(jnp.int32, (tm, tm), 1)
    strict_lower = jnp.where(c < r, 1.0, 0.0).astype(BF16)
    rank = jnp.dot(strict_lower, sel.astype(BF16), preferred_element_type=F32) + carry_ref[...]
    carry_ref[...] = carry_ref[...] + jnp.sum(sel, axis=0, keepdims=True)
    cnt_ref[...] = carry_ref[...]

    out_lane = lax.broadcasted_iota(jnp.int32, (tm, LANES), 1)
    idx_out = jnp.zeros((tm, LANES), jnp.int32)
    wt_out = jnp.zeros((tm, LANES), F32)
    rank_out = jnp.zeros((tm, LANES), jnp.int32)
    for k in range(TOP_K):
        rk = jnp.sum(jnp.where(lane == idxs[k], rank, 0.0), axis=-1, keepdims=True)
        idx_out = jnp.where(out_lane == k, idxs[k], idx_out)
        wt_out = jnp.where(out_lane == k, wts[k] / wsum * ROUTED_SCALE, wt_out)
        rank_out = jnp.where(out_lane == k, rk.astype(jnp.int32), rank_out)
    idx_ref[...] = idx_out
    wt_ref[...] = wt_out
    rank_ref[...] = rank_out


def _router(h, w_router, e_bias):
    s, d = h.shape
    e = w_router.shape[1]
    tm = ROUTER_TM
    row = pl.BlockSpec((tm, LANES), lambda i: (i, 0))
    return pl.pallas_call(
        _router_kernel,
        grid=(s // tm,),
        in_specs=[
            pl.BlockSpec((tm, d), lambda i: (i, 0)),
            pl.BlockSpec((d, e), lambda i: (0, 0)),
            pl.BlockSpec((1, e), lambda i: (0, 0)),
        ],
        out_specs=[row, row, row, pl.BlockSpec((1, e), lambda i: (0, 0))],
        out_shape=[
            jax.ShapeDtypeStruct((s, LANES), jnp.int32),
            jax.ShapeDtypeStruct((s, LANES), F32),
            jax.ShapeDtypeStruct((s, LANES), jnp.int32),
            jax.ShapeDtypeStruct((1, e), F32),
        ],
        scratch_shapes=[pltpu.VMEM((1, e), F32)],
        compiler_params=_cparams(("arbitrary",)),
        name="router_topk",
    )(h, w_router, e_bias)


def _dest_kernel(idx_ref, rank_ref, ps_ref, dest_ref):
    idx = idx_ref[...]
    rank = rank_ref[...]
    ps = ps_ref[...]
    lane = lax.broadcasted_iota(jnp.int32, (idx.shape[0], ps.shape[1]), 1)
    out_lane = lax.broadcasted_iota(jnp.int32, idx.shape, 1)
    dest = jnp.zeros(idx.shape, jnp.int32)
    for k in range(TOP_K):
        start_k = jnp.sum(jnp.where(lane == idx[:, k:k + 1], ps, 0.0), axis=-1, keepdims=True)
        dest = jnp.where(out_lane == k, start_k.astype(jnp.int32) + rank, dest)
    dest_ref[...] = dest.T[:TOP_K, :]


def _dest(idx, rank, pstarts):
    s = idx.shape[0]
    e = pstarts.shape[1]
    tm = ROUTER_TM
    row = pl.BlockSpec((tm, LANES), lambda i: (i, 0))
    return pl.pallas_call(
        _dest_kernel,
        grid=(s // tm,),
        in_specs=[row, row, pl.BlockSpec((1, e), lambda i: (0, 0))],
        out_specs=pl.BlockSpec((TOP_K, tm), lambda i: (0, i)),
        out_shape=jax.ShapeDtypeStruct((TOP_K, s), jnp.int32),
        compiler_params=_cparams(("arbitrary",)),
        name="moe_dest",
    )(idx, rank, pstarts)


def _pack_rows(x):
    half = x.shape[1] // 2
    lo = lax.bitcast_convert_type(x[:, :half].astype(BF16).astype(F32), U32) >> 16
    hi = lax.bitcast_convert_type(x[:, half:].astype(BF16).astype(F32), U32) & HI_MASK
    return lo | hi


def _unpack_rows(w):
    lo = lax.bitcast_convert_type(w << 16, F32)
    hi = lax.bitcast_convert_type(w & HI_MASK, F32)
    return lo, hi


def _dispatch_kernel(dest_ref, h_ref, xs_ref, hp_ref, sem):
    tt = DISPATCH_TT
    hp_ref[...] = _pack_rows(h_ref[...])

    def row_copy(t, k):
        return pltpu.make_async_copy(hp_ref.at[pl.ds(t, 1)], xs_ref.at[pl.ds(dest_ref[k, t], 1)], sem)

    def issue(t, carry):
        for k in range(TOP_K):
            row_copy(t, k).start()
        return carry

    def drain(t, carry):
        for k in range(TOP_K):
            row_copy(t, k).wait()
        return carry

    lax.fori_loop(0, tt, issue, 0)
    lax.fori_loop(0, tt, drain, 0)


def _dispatch(dest_t, h, n_rows):
    s, d = h.shape
    tt = DISPATCH_TT
    return pl.pallas_call(
        _dispatch_kernel,
        grid=(s // tt,),
        in_specs=[
            pl.BlockSpec((TOP_K, tt), lambda i: (0, i), memory_space=pltpu.SMEM),
            pl.BlockSpec((tt, d), lambda i: (i, 0)),
        ],
        out_specs=pl.BlockSpec(memory_space=pl.ANY),
        out_shape=jax.ShapeDtypeStruct((n_rows, d // 2), U32),
        scratch_shapes=[pltpu.VMEM((tt, d // 2), U32), pltpu.SemaphoreType.DMA(())],
        compiler_params=_cparams(("arbitrary",)),
        name="moe_dispatch",
    )(dest_t, h)


def _expert_kernel(start_ref, cnt_ref, xs_ref, wg_ref, wu_ref, wd_ref, y_ref,
                   wgb_ref, wub_ref, wdb_ref, xin_ref, yout_ref, in_sem, out_sem, st_ref):
    e = pl.program_id(0)
    ne = pl.num_programs(0)
    rb, ch = MOE_RB, MOE_CH

    def n_chunks(cnt_e, j):
        return jnp.clip((cnt_e - j * rb + ch - 1) // ch, 0, rb // ch)

    def in_chunk(row0, slot, c):
        return pltpu.make_async_copy(xs_ref.at[pl.ds(pl.multiple_of(row0 + c * ch, ROW_ALIGN), ch)],
                                     xin_ref.at[slot, pl.ds(pl.multiple_of(c * ch, ch), ch)], in_sem.at[slot])

    def out_chunk(row0, slot, c):
        return pltpu.make_async_copy(yout_ref.at[slot, pl.ds(pl.multiple_of(c * ch, ch), ch)],
                                     y_ref.at[pl.ds(pl.multiple_of(row0 + c * ch, ROW_ALIGN), ch)], out_sem.at[slot])

    def for_chunks(n, fn):
        def body(c, carry):
            fn(c)
            return carry
        lax.fori_loop(0, n, body, 0)

    @pl.when(e == 0)
    def _():
        st_ref[0] = 0
        st_ref[1] = 0
        st_ref[2] = 0
        for_chunks(n_chunks(cnt_ref[0], 0), lambda c: in_chunk(start_ref[0], 0, c).start())

    wgb_ref[...] = wg_ref[0].astype(BF16)
    wub_ref[...] = wu_ref[0].astype(BF16)
    wdb_ref[...] = wd_ref[0].astype(BF16)

    start = start_ref[e]
    cnt = cnt_ref[e]
    n_sub = jnp.maximum(1, (cnt + rb - 1) // rb)
    e_next = jnp.minimum(e + 1, ne - 1)
    next_start = start_ref[e_next]
    next_first = jnp.where(e == ne - 1, 0, n_chunks(cnt_ref[e_next], 0))

    def sub_block(j, carry):
        slot = st_ref[0]
        row0 = start + j * rb
        n_cur = n_chunks(cnt, j)
        last_j = j == n_sub - 1
        nxt_row = jnp.where(last_j, next_start, row0 + rb)
        n_nxt = jnp.where(last_j, next_first, n_chunks(cnt, j + 1))
        for_chunks(n_nxt, lambda c: in_chunk(nxt_row, 1 - slot, c).start())
        for_chunks(n_cur, lambda c: in_chunk(row0, slot, c).wait())

        xw = xin_ref[slot]
        live = (lax.broadcasted_iota(jnp.int32, xw.shape, 0) + j * rb) < cnt
        lo, hi = _unpack_rows(jnp.where(live, xw, jnp.zeros_like(xw)))
        xb = jnp.concatenate([lo.astype(BF16), hi.astype(BF16)], axis=1)
        gate = jnp.dot(xb, wgb_ref[...], preferred_element_type=F32)
        up = jnp.dot(xb, wub_ref[...], preferred_element_type=F32)
        hdn = (_silu(gate) * up).astype(BF16)
        yout_ref[slot] = _pack_rows(jnp.dot(hdn, wdb_ref[...], preferred_element_type=F32))

        prev_row, prev_n = st_ref[2], st_ref[1]
        for_chunks(prev_n, lambda c: out_chunk(prev_row, 1 - slot, c).wait())
        for_chunks(n_cur, lambda c: out_chunk(row0, slot, c).start())
        st_ref[0] = 1 - slot
        st_ref[1] = n_cur
        st_ref[2] = row0
        return carry

    lax.fori_loop(0, n_sub, sub_block, 0)

    @pl.when(e == ne - 1)
    def _():
        prev_row, prev_n, prev_slot = st_ref[2], st_ref[1], 1 - st_ref[0]
        for_chunks(prev_n, lambda c: out_chunk(prev_row, prev_slot, c).wait())


def _experts(starts, counts, xs, w_gate, w_up, w_down):
    r, dh = xs.shape
    ne, d, f = w_gate.shape
    rb = MOE_RB
    wsel = lambda e, st, ct: (e, 0, 0)
    grid_spec = pltpu.PrefetchScalarGridSpec(
        num_scalar_prefetch=2,
        grid=(ne,),
        in_specs=[
            pl.BlockSpec(memory_space=pl.ANY),
            pl.BlockSpec((1, d, f), wsel),
            pl.BlockSpec((1, d, f), wsel),
            pl.BlockSpec((1, f, d), wsel),
        ],
        out_specs=pl.BlockSpec(memory_space=pl.ANY),
        scratch_shapes=[
            pltpu.VMEM((d, f), BF16), pltpu.VMEM((d, f), BF16), pltpu.VMEM((f, d), BF16),
            pltpu.VMEM((2, rb, dh), U32), pltpu.VMEM((2, rb, dh), U32),
            pltpu.SemaphoreType.DMA((2,)), pltpu.SemaphoreType.DMA((2,)),
            pltpu.SMEM((3,), jnp.int32),
        ],
    )
    return pl.pallas_call(
        _expert_kernel,
        grid_spec=grid_spec,
        out_shape=jax.ShapeDtypeStruct((r, dh), U32),
        compiler_params=_cparams(("arbitrary",)),
        name="moe_experts",
    )(starts, counts, xs, w_gate, w_up, w_down)


def _combine_kernel(dest_ref, y_ref, wt_ref, h_ref, wg_ref, wu_ref, wd_ref, g_ref, b_ref, o_ref,
                    rows_ref, wgb_ref, wub_ref, wdb_ref, sem):
    tt = COMBINE_TT

    @pl.when(pl.program_id(0) == 0)
    def _():
        wgb_ref[...] = wg_ref[...].astype(BF16)
        wub_ref[...] = wu_ref[...].astype(BF16)
        wdb_ref[...] = wd_ref[...].astype(BF16)

    def row_copy(t, k):
        return pltpu.make_async_copy(y_ref.at[pl.ds(dest_ref[k, t], 1)], rows_ref.at[k, pl.ds(t, 1)], sem)

    def issue(t, carry):
        for k in range(TOP_K):
            row_copy(t, k).start()
        return carry

    def drain(t, carry):
        for k in range(TOP_K):
            row_copy(t, k).wait()
        return carry

    lax.fori_loop(0, tt, issue, 0)

    h = h_ref[...]
    hb = h.astype(BF16)
    gate = jnp.dot(hb, wgb_ref[...], preferred_element_type=F32)
    up = jnp.dot(hb, wub_ref[...], preferred_element_type=F32)
    hdn = (_silu(gate) * up).astype(BF16)
    shared = jnp.dot(hdn, wdb_ref[...], preferred_element_type=F32)

    lax.fori_loop(0, tt, drain, 0)

    wt = wt_ref[...]
    lo, hi = _unpack_rows(rows_ref[0])
    r_lo, r_hi = lo * wt[:, 0:1], hi * wt[:, 0:1]
    for k in range(1, TOP_K):
        lo, hi = _unpack_rows(rows_ref[k])
        r_lo = r_lo + lo * wt[:, k:k + 1]
        r_hi = r_hi + hi * wt[:, k:k + 1]
    ff = jnp.concatenate([r_lo, r_hi], axis=1) + shared
    o_ref[...] = _layer_norm(ALPHA * h + ff, g_ref[...], b_ref[...])


def _combine(dest_t, y, wt, h, ws_gate, ws_up, ws_down, g, b):
    s, d = h.shape
    f = ws_gate.shape[1]
    tt = COMBINE_TT
    row = pl.BlockSpec((tt, d), lambda i: (i, 0))
    vec = pl.BlockSpec((1, d), lambda i: (0, 0))
    return pl.pallas_call(
        _combine_kernel,
        grid=(s // tt,),
        in_specs=[
            pl.BlockSpec((TOP_K, tt), lambda i: (0, i), memory_space=pltpu.SMEM),
            pl.BlockSpec(memory_space=pl.ANY),
            pl.BlockSpec((tt, LANES), lambda i: (i, 0)),
            row,
            pl.BlockSpec((d, f), lambda i: (0, 0)),
            pl.BlockSpec((d, f), lambda i: (0, 0)),
            pl.BlockSpec((f, d), lambda i: (0, 0)),
            vec, vec,
        ],
        out_specs=row,
        out_shape=jax.ShapeDtypeStruct((s, d), F32),
        scratch_shapes=[pltpu.VMEM((TOP_K, tt, d // 2), U32),
                        pltpu.VMEM((d, f), BF16), pltpu.VMEM((d, f), BF16), pltpu.VMEM((f, d), BF16),
                        pltpu.SemaphoreType.DMA(())],
        compiler_params=_cparams(("arbitrary",)),
        name="moe_combine_ln2",
    )(dest_t, y, wt, h, ws_gate, ws_up, ws_down, g, b)


def _rope_tables(positions):
    half = DA_QK_DIM // 2
    inv = ROPE_THETA ** (-jnp.arange(0, DA_QK_DIM, 2, dtype=F32) / DA_QK_DIM)
    ang = positions.astype(F32)[:, None] * inv
    cos = jnp.tile(jnp.cos(ang), (1, LANES // half))
    sin = jnp.sin(ang)
    sin = jnp.tile(jnp.concatenate([-sin, sin], axis=-1), (1, LANES // DA_QK_DIM))
    return cos, sin


def _moe(h, w_router, e_bias, w_gate, w_up, w_down, ws_gate, ws_up, ws_down, ln_g, ln_b):
    s, d = h.shape
    e = w_router.shape[1]
    idx, wt, rank, counts = _router(h, w_router, e_bias.reshape(1, e))
    counts = counts[0].astype(jnp.int32)
    acounts = (counts + ROW_ALIGN - 1) // ROW_ALIGN * ROW_ALIGN
    ends = jnp.cumsum(acounts)
    starts = (ends - acounts).astype(jnp.int32)
    dest_t = _dest(idx, rank, starts.astype(F32).reshape(1, e))
    n_rows = s * TOP_K + e * ROW_ALIGN + MOE_RB

    xs = _dispatch(dest_t, h, n_rows)
    y = _experts(starts, counts, xs, w_gate, w_up, w_down)
    return _combine(dest_t, y, wt, h, ws_gate, ws_up, ws_down, ln_g.reshape(1, d), ln_b.reshape(1, d))


def kernel(x, positions, ln_in_g, ln_in_b, w_in, w_out, lam_q1, lam_k1, lam_q2, lam_k2, da_norm_g,
           hg_lb_logits, hg_norm_g, ln1_g, ln1_b, w_router, e_bias, w_gate, w_up, w_down,
           ws_gate, ws_up, ws_down, ln2_g, ln2_b):
    bsz, s, d = x.shape
    assert bsz == 1 and w_in.shape[0] == DEPTH
    x2 = x.reshape(s, d)
    cos, sin = _rope_tables(positions.reshape(s))
    lower_bounds = jnp.cumsum(jax.nn.softmax(hg_lb_logits.astype(F32), axis=0), axis=0)
    lam = (jnp.exp(jnp.sum(lam_q1[0] * lam_k1[0])) - jnp.exp(jnp.sum(lam_q2[0] * lam_k2[0]))
           + LAMBDA_INIT).reshape(1).astype(F32)
    gi, bi = ln_in_g.reshape(1, d), ln_in_b.reshape(1, d)

    proj = _inproj(x2, gi, bi, w_in[0])
    qm, kr, vb = _rope(proj, cos, sin)
    y_a = _attention(lam, qm, kr, vb, da_norm_g[0].reshape(1, LANES))
    hg_col0 = 3 * DA_HEADS
    y_b = _hgrn(proj, lower_bounds[0].reshape(1, -1), hg_norm_g[0].reshape(1, LANES), hg_col0)
    mix_in = jnp.concatenate([y_a, y_b], axis=-1)
    h1 = _outproj(mix_in, w_out[0], x2, gi, bi, ln1_g[0].reshape(1, d), ln1_b[0].reshape(1, d))
    h2 = _moe(h1, w_router[0], e_bias[0], w_gate[0], w_up[0], w_down[0],
              ws_gate[0], ws_up[0], ws_down[0], ln2_g[0], ln2_b[0])
    return h2.reshape(bsz, s, d)
```

```python
import functools
import math

import jax
import jax.numpy as jnp
import numpy as np
from jax import lax
from jax.experimental import pallas as pl
from jax.experimental.pallas import tpu as pltpu

F32 = jnp.float32
BF16 = jnp.bfloat16
U32 = jnp.uint32
I32 = jnp.int32
HI_MASK = np.uint32(0xFFFF0000)

CHUNK = 64
DA_HEADS = 8
DA_QK_DIM = 64
DA_V_DIM = 2 * DA_QK_DIM
HG_HEADS = 8
HG_DIM = 128
ROPE_THETA = 10000.0
N_EXPERTS = 256
TOP_K = 8
N_GROUPS = 8
TOPK_GROUPS = 4
GROUP_SIZE = N_EXPERTS // N_GROUPS
ROUTED_SCALE = 2.5
DEPTH = 1
ALPHA = (2.0 * DEPTH) ** 0.25
LN_EPS = 1e-5
RMS_EPS = 1e-5
LAMBDA_INIT = 0.8 - 0.6 * math.exp(-0.3 * 0)

LANES = 128
VMEM_LIMIT = 56 * 1024 * 1024
NEG = -1e30

LN_TM = 512
INPROJ_TM, INPROJ_TN = 1024, 512
ATTN_T = 1024
HG_TR = 512
HG_SUB = 16
HG_GRP = 128
OUT_TM, OUT_TK = 512, 512
ROUTER_TM = 512
MOE_RB = 256
MOE_CH = 64
SUBLANES = 8
ROW_ALIGN = SUBLANES
TAG_EXPERT0 = 8
DISPATCH_TT = 256
COMBINE_TT = 128


def _cparams(sem):
    return pltpu.CompilerParams(dimension_semantics=sem, vmem_limit_bytes=VMEM_LIMIT)


def _layer_norm(xf, g, b):
    mu = jnp.mean(xf, axis=-1, keepdims=True)
    xc = xf - mu
    var = jnp.mean(xc * xc, axis=-1, keepdims=True)
    return xc * lax.rsqrt(var + LN_EPS) * g + b


def _silu(x):
    return x * jax.nn.sigmoid(x)


def _ln_in_kernel(x_ref, g_ref, b_ref, o_ref):
    o_ref[...] = _layer_norm(x_ref[...], g_ref[...], b_ref[...]).astype(o_ref.dtype)


def _ln_in(x, g, b):
    s, d = x.shape
    tm = LN_TM
    vec = pl.BlockSpec((1, d), lambda i: (0, 0))
    return pl.pallas_call(
        _ln_in_kernel,
        grid=(s // tm,),
        in_specs=[pl.BlockSpec((tm, d), lambda i: (i, 0)), vec, vec],
        out_specs=pl.BlockSpec((tm, d), lambda i: (i, 0)),
        out_shape=jax.ShapeDtypeStruct((s, d), BF16),
        compiler_params=_cparams(("arbitrary",)),
        name="ln_in",
    )(x, g, b)


def _rope(x, cos, sin, first_half):
    rot = jnp.where(first_half, pltpu.roll(x, LANES - DA_QK_DIM // 2, 1), pltpu.roll(x, DA_QK_DIM // 2, 1))
    return x * cos + rot * sin


def _proj_kernel(role, h_ref, w_ref, *rest):
    acc = jnp.dot(h_ref[...], w_ref[...], preferred_element_type=F32)
    if role == "plain":
        (o_ref,) = rest
        o_ref[...] = acc
        return
    heads = acc.shape[1] // LANES
    if role == "v":
        (o_ref,) = rest
        for hh in range(heads):
            o_ref[hh] = acc[:, hh * LANES:(hh + 1) * LANES].astype(BF16)
        return
    cos_ref, sin_ref, o_ref = rest
    cos = cos_ref[...]
    sin = sin_ref[...]
    lane = lax.broadcasted_iota(jnp.int32, cos.shape, 1)
    first_half = (lane % DA_QK_DIM) < (DA_QK_DIM // 2)
    map0 = lane < DA_QK_DIM
    for hh in range(heads):
        r = _rope(acc[:, hh * LANES:(hh + 1) * LANES], cos, sin, first_half)
        if role == "k":
            o_ref[hh] = r.astype(BF16)
        else:
            r = r * (DA_QK_DIM ** -0.5)
            o_ref[hh, 0] = jnp.where(map0, r, 0.0).astype(BF16)
            o_ref[hh, 1] = jnp.where(map0, 0.0, r).astype(BF16)


def _proj(role, hb, w, col0, ncols, cos=None, sin=None):
    s, d = hb.shape
    tm, tn = INPROJ_TM, INPROJ_TN
    jb = col0 // tn
    hpt = tn // LANES
    nh = ncols // LANES
    in_specs = [pl.BlockSpec((tm, d), lambda i, j: (i, 0)), pl.BlockSpec((d, tn), lambda i, j: (0, jb + j))]
    args = [hb, w]
    if role in ("q", "k"):
        tab = pl.BlockSpec((tm, LANES), lambda i, j: (i, 0))
        in_specs += [tab, tab]
        args += [cos, sin]
    if role == "plain":
        out_spec = pl.BlockSpec((tm, tn), lambda i, j: (i, j))
        out_shape = jax.ShapeDtypeStruct((s, ncols), F32)
    elif role == "q":
        out_spec = pl.BlockSpec((hpt, 2, tm, LANES), lambda i, j: (j, 0, i, 0))
        out_shape = jax.ShapeDtypeStruct((nh, 2, s, LANES), BF16)
    else:
        out_spec = pl.BlockSpec((hpt, tm, LANES), lambda i, j: (j, i, 0))
        out_shape = jax.ShapeDtypeStruct((nh, s, LANES), BF16)
    return pl.pallas_call(
        functools.partial(_proj_kernel, role),
        grid=(s // tm, ncols // tn),
        in_specs=in_specs,
        out_specs=out_spec,
        out_shape=out_shape,
        compiler_params=_cparams(("arbitrary", "arbitrary")),
        name="inproj_" + role,
    )(*args)


def _attn_kernel(lam_ref, q_ref, k_ref, v_ref, g_ref, o_ref, acc_ref, m_ref, l_ref):
    t = ATTN_T
    i = pl.program_id(1)
    hf = t // 2
    qa = q_ref[0]
    q = jnp.concatenate([qa[0, :hf], qa[1, :hf], qa[0, hf:], qa[1, hf:]], axis=0)
    m_ref[...] = jnp.full(m_ref.shape, NEG, F32)
    l_ref[...] = jnp.zeros(l_ref.shape, F32)
    acc_ref[...] = jnp.zeros(acc_ref.shape, F32)

    def step(start, width, rows, masked):
        k = k_ref[0, pl.ds(start, width), :]
        v = v_ref[0, pl.ds(start, width), :]
        s = lax.dot_general(q[rows], k, (((1,), (1,)), ((), ())), preferred_element_type=F32)
        if masked:
            row = lax.broadcasted_iota(jnp.int32, s.shape, 0) % hf
            col = lax.broadcasted_iota(jnp.int32, s.shape, 1)
            s = jnp.where((col // CHUNK) <= (row // CHUNK), s, NEG)
        cols = [s[:, j * LANES:(j + 1) * LANES] for j in range(width // LANES)]
        mx = cols[0]
        for cj in cols[1:]:
            mx = jnp.maximum(mx, cj)
        m_prev = m_ref[rows, :]
        m_new = jnp.maximum(m_prev, jnp.max(mx, axis=-1, keepdims=True))
        alpha = jnp.exp(m_prev - m_new)
        ps = [jnp.exp(cj - m_new) for cj in cols]
        lsum = ps[0]
        for pj in ps[1:]:
            lsum = lsum + pj
        l_ref[rows, :] = alpha * l_ref[rows, :] + lsum
        m_ref[rows, :] = m_new
        p = jnp.concatenate([pj.astype(BF16) for pj in ps], axis=-1)
        acc_ref[rows, :] = alpha * acc_ref[rows, :] + jnp.dot(p, v, preferred_element_type=F32)

    def body(c, carry):
        step(pl.multiple_of(c * t, t), t, slice(0, 2 * t), False)
        return carry

    lax.fori_loop(0, i, body, 0)
    d0 = pl.multiple_of(i * t, t)
    step(d0, hf, slice(0, t), True)
    step(d0, hf, slice(t, 2 * t), False)
    step(pl.multiple_of(d0 + hf, hf), hf, slice(t, 2 * t), True)

    lam = lam_ref[0]
    acc = acc_ref[...]
    l = jnp.sum(l_ref[...], axis=-1, keepdims=True)
    o1 = jnp.concatenate([acc[:hf] / l[:hf], acc[t:t + hf] / l[t:t + hf]], axis=0)
    o2 = jnp.concatenate([acc[hf:t] / l[hf:t], acc[t + hf:] / l[t + hf:]], axis=0)
    o = o1 - lam * o2
    o = o * lax.rsqrt(jnp.mean(o * o, axis=-1, keepdims=True) + RMS_EPS) * g_ref[...] * (1.0 - LAMBDA_INIT)
    o_ref[...] = o.astype(o_ref.dtype)


def _attention(lam, qm, kr, vb, norm_g):
    h, _, s, _ = qm.shape
    t = ATTN_T
    return pl.pallas_call(
        _attn_kernel,
        grid=(h, s // t),
        in_specs=[
            pl.BlockSpec(memory_space=pltpu.SMEM),
            pl.BlockSpec((1, 2, t, LANES), lambda hh, i: (hh, 0, i, 0)),
            pl.BlockSpec((1, s, LANES), lambda hh, i: (hh, 0, 0)),
            pl.BlockSpec((1, s, LANES), lambda hh, i: (hh, 0, 0)),
            pl.BlockSpec((1, LANES), lambda hh, i: (0, 0)),
        ],
        out_specs=pl.BlockSpec((t, LANES), lambda hh, i: (i, hh)),
        out_shape=jax.ShapeDtypeStruct((s, h * LANES), BF16),
        scratch_shapes=[
            pltpu.VMEM((2 * t, LANES), F32),
            pltpu.VMEM((2 * t, LANES), F32),
            pltpu.VMEM((2 * t, LANES), F32),
        ],
        compiler_params=_cparams(("arbitrary", "arbitrary")),
        name="diff_attn",
    )(lam, qm, kr, vb, norm_g)


def _hgrn_kernel(q_ref, f_ref, i_ref, g_ref, lb_ref, ng_ref, o_ref,
                 st_ref, qs_ref, kk_ref, b_ref, kh_ref, od_ref, iv_ref):
    tr, sub, grp = HG_TR, HG_SUB, HG_GRP

    @pl.when(pl.program_id(1) == 0)
    def _():
        st_ref[...] = jnp.zeros(st_ref.shape, F32)

    lb = lb_ref[...]
    f = lb + (1.0 - lb) * jax.nn.sigmoid(f_ref[...])
    log_f = jnp.log(f)
    kk = 1.0 - f
    qs_ref[...] = _silu(q_ref[...])
    zpad = jnp.zeros((sub, LANES), F32)
    kk_ref[0:sub, :] = zpad
    b_ref[0:sub, :] = zpad
    iv_ref[0:sub, :] = zpad
    kk_ref[sub:, :] = kk
    iv_ref[sub:, :] = i_ref[...]

    r = lax.broadcasted_iota(jnp.int32, (grp, grp), 0)
    c = lax.broadcasted_iota(jnp.int32, (grp, grp), 1)
    same = (r // sub) == (c // sub)
    tri = jnp.where(same & (c <= r), 1.0, 0.0).astype(F32)
    blk = jnp.where(same, 1.0, 0.0).astype(F32)
    grp_iota = lax.broadcasted_iota(jnp.int32, (grp, LANES), 0)
    pos_in_sub = grp_iota % sub

    for gi in range(tr // grp):
        g0 = gi * grp
        rows = slice(g0, g0 + grp)
        prow = slice(sub + g0, sub + g0 + grp)
        lf = log_f[rows]
        bg = jnp.dot(tri, lf, preferred_element_type=F32, precision=lax.Precision.HIGHEST)
        tot = jnp.dot(blk, lf, preferred_element_type=F32, precision=lax.Precision.HIGHEST)
        b_ref[prow, :] = bg
        kh_ref[rows, :] = kk[rows] * jnp.exp(tot - bg)
        qg = qs_ref[rows, :]
        acc = jnp.sum(qg * kk[rows], axis=-1, keepdims=True) * i_ref[rows, :]
        for lag in range(1, sub):
            srow = slice(sub + g0 - lag, sub + g0 - lag + grp)
            e = jnp.exp(jnp.where(pos_in_sub >= lag, bg - b_ref[srow, :], NEG))
            w = jnp.sum(qg * kk_ref[srow, :] * e, axis=-1, keepdims=True)
            acc = acc + w * iv_ref[srow, :]
        od_ref[rows, :] = acc

    st = st_ref[...]
    for gi in range(tr // grp):
        g0 = gi * grp
        iv_t = i_ref[g0:g0 + grp, :].T.astype(BF16)
        kh_g = kh_ref[g0:g0 + grp, :]
        upds = []
        for j in range(grp // sub):
            in_sub = (grp_iota >= j * sub) & (grp_iota < (j + 1) * sub)
            upds.append(jnp.dot(iv_t, jnp.where(in_sub, kh_g, 0.0).astype(BF16), preferred_element_type=F32))
        for j in range(grp // sub):
            base = g0 + j * sub
            bj = b_ref[sub + base:sub + base + sub, :]
            qj = qs_ref[base:base + sub, :]
            o_inter = lax.dot_general((qj * jnp.exp(bj)).astype(BF16), st.astype(BF16),
                                      (((1,), (1,)), ((), ())), preferred_element_type=F32)
            od_ref[base:base + sub, :] = od_ref[base:base + sub, :] + o_inter
            st = st * jnp.exp(bj[sub - 1:sub]) + upds[j]
    st_ref[...] = st

    o = od_ref[...]
    o = o * lax.rsqrt(jnp.mean(o * o, axis=-1, keepdims=True) + RMS_EPS) * ng_ref[...]
    o_ref[...] = (o * _silu(g_ref[...])).astype(o_ref.dtype)


def _hgrn(proj, lower_bound, norm_g, col0):
    s = proj.shape[0]
    tr, h = HG_TR, HG_HEADS
    blk = lambda off: pl.BlockSpec((tr, LANES), lambda hh, i: (i, col0 + off + hh))
    pad = pltpu.VMEM((tr + HG_SUB, LANES), F32)
    full = pltpu.VMEM((tr, LANES), F32)
    return pl.pallas_call(
        _hgrn_kernel,
        grid=(h, s // tr),
        in_specs=[blk(0), blk(h), blk(2 * h), blk(3 * h),
                  pl.BlockSpec((1, LANES), lambda hh, i: (0, hh)),
                  pl.BlockSpec((1, LANES), lambda hh, i: (0, 0))],
        out_specs=pl.BlockSpec((tr, LANES), lambda hh, i: (i, hh)),
        out_shape=jax.ShapeDtypeStruct((s, h * LANES), BF16),
        scratch_shapes=[pltpu.VMEM((HG_DIM, HG_DIM), F32), full, pad, pad, full, full, pad],
        compiler_params=_cparams(("arbitrary", "arbitrary")),
        name="hgrn2",
    )(proj, proj, proj, proj, lower_bound, norm_g)


def _outproj_kernel(ya_ref, yb_ref, w_ref, x_ref, gi_ref, bi_ref, g1_ref, b1_ref, h_ref, acc_ref):
    k = pl.program_id(1)
    nk = pl.num_programs(1)

    @pl.when(k == 0)
    def _():
        acc_ref[...] = jnp.zeros(acc_ref.shape, F32)

    w = w_ref[...]

    @pl.when(k < nk // 2)
    def _():
        acc_ref[...] += jnp.dot(ya_ref[...], w, preferred_element_type=F32)

    @pl.when(k >= nk // 2)
    def _():
        acc_ref[...] += jnp.dot(yb_ref[...], w, preferred_element_type=F32)

    @pl.when(k == nk - 1)
    def _():
        h0 = _layer_norm(x_ref[...], gi_ref[...], bi_ref[...])
        h_ref[...] = _layer_norm(ALPHA * h0 + acc_ref[...], g1_ref[...], b1_ref[...])


def _outproj(ya, yb, w, x, gi, bi, g1, b1):
    s, d = x.shape
    tm, tk = OUT_TM, OUT_TK
    nka = ya.shape[1] // tk
    nk = w.shape[0] // tk
    assert nk == 2 * nka and yb.shape[1] == ya.shape[1]
    vec = pl.BlockSpec((1, d), lambda i, k: (0, 0))
    return pl.pallas_call(
        _outproj_kernel,
        grid=(s // tm, nk),
        in_specs=[
            pl.BlockSpec((tm, tk), lambda i, k: (i, jnp.minimum(k, nka - 1))),
            pl.BlockSpec((tm, tk), lambda i, k: (i, jnp.maximum(k - nka, 0))),
            pl.BlockSpec((tk, d), lambda i, k: (k, 0)),
            pl.BlockSpec((tm, d), lambda i, k: (i, 0)),
            vec, vec, vec, vec,
        ],
        out_specs=pl.BlockSpec((tm, d), lambda i, k: (i, 0)),
        out_shape=jax.ShapeDtypeStruct((s, d), F32),
        scratch_shapes=[pltpu.VMEM((tm, d), F32)],
        compiler_params=_cparams(("arbitrary", "arbitrary")),
        name="outproj_ln1",
    )(ya, yb, w, x, gi, bi, g1, b1)


def _router_kernel(h_ref, w_ref, eb_ref, idx_ref, wt_ref, rank_ref, cnt_ref, carry_ref):
    tm = ROUTER_TM
    i = pl.program_id(0)

    @pl.when(i == 0)
    def _():
        carry_ref[...] = jnp.zeros(carry_ref.shape, F32)

    logits = jnp.dot(h_ref[...], w_ref[...], preferred_element_type=F32, precision=lax.Precision.HIGHEST)
    scores = jax.nn.sigmoid(logits)
    choice = scores + eb_ref[...]
    lane = lax.broadcasted_iota(jnp.int32, choice.shape, 1)
    grp_of_lane = lane // GROUP_SIZE
    big = jnp.int32(N_EXPERTS)
    ninf = -jnp.inf

    def first_argmax(vals):
        m = jnp.max(vals, axis=-1, keepdims=True)
        idx = jnp.min(jnp.where(vals == m, lane, big), axis=-1, keepdims=True)
        return m, idx

    gs = []
    for g in range(N_GROUPS):
        vals = jnp.where(grp_of_lane == g, choice, ninf)
        m1, i1 = first_argmax(vals)
        m2 = jnp.max(jnp.where(lane == i1, ninf, vals), axis=-1, keepdims=True)
        gs.append(m1 + m2)
    allowed = jnp.zeros(choice.shape, jnp.int32)
    for g in range(N_GROUPS):
        beats = jnp.zeros(gs[g].shape, jnp.int32)
        for g2 in range(N_GROUPS):
            if g2 == g:
                continue
            better = (gs[g2] >= gs[g]) if g2 < g else (gs[g2] > gs[g])
            beats = beats + jnp.where(better, 1, 0)
        allowed = jnp.where(grp_of_lane == g, jnp.where(beats < TOPK_GROUPS, 1, 0), allowed)
    masked = jnp.where(allowed > 0, choice, ninf)

    sel = jnp.zeros(choice.shape, F32)
    idxs, wts = [], []
    for _ in range(TOP_K):
        _, ik = first_argmax(masked)
        hit = lane == ik
        idxs.append(ik)
        wts.append(jnp.sum(jnp.where(hit, scores, 0.0), axis=-1, keepdims=True))
        sel = jnp.where(hit, 1.0, sel)
        masked = jnp.where(hit, ninf, masked)
    wsum = wts[0]
    for k in range(1, TOP_K):
        wsum = wsum + wts[k]

    r = lax.broadcasted_iota(jnp.int32, (tm, tm), 0)
    c = lax.broadcasted_iota(jnp.int32, (tm, tm), 1)
    strict_lower = jnp.where(c < r, 1.0, 0.0).astype(BF16)
    rank = jnp.dot(strict_lower, sel.astype(BF16), preferred_element_type=F32) + carry_ref[...]
    carry_ref[...] = carry_ref[...] + jnp.sum(sel, axis=0, keepdims=True)
    cnt_ref[...] = carry_ref[...]

    out_lane = lax.broadcasted_iota(jnp.int32, (tm, LANES), 1)
    idx_out = jnp.zeros((tm, LANES), jnp.int32)
    wt_out = jnp.zeros((tm, LANES), F32)
    rank_out = jnp.zeros((tm, LANES), jnp.int32)
    for k in range(TOP_K):
        rk = jnp.sum(jnp.where(lane == idxs[k], rank, 0.0), axis=-1, keepdims=True)
        idx_out = jnp.where(out_lane == k, idxs[k], idx_out)
        wt_out = jnp.where(out_lane == k, wts[k] / wsum * ROUTED_SCALE, wt_out)
        rank_out = jnp.where(out_lane == k, rk.astype(jnp.int32), rank_out)
    idx_ref[...] = idx_out
    wt_ref[...] = wt_out
    rank_ref[...] = rank_out


def _router(h, w_router, e_bias):
    s, d = h.shape
    e = w_router.shape[1]
    tm = ROUTER_TM
    row = pl.BlockSpec((tm, LANES), lambda i: (i, 0))
    return pl.pallas_call(
        _router_kernel,
        grid=(s // tm,),
        in_specs=[
            pl.BlockSpec((tm, d), lambda i: (i, 0)),
            pl.BlockSpec((d, e), lambda i: (0, 0)),
            pl.BlockSpec((1, e), lambda i: (0, 0)),
        ],
        out_specs=[row, row, row, pl.BlockSpec((1, e), lambda i: (0, 0))],
        out_shape=[
            jax.ShapeDtypeStruct((s, LANES), jnp.int32),
            jax.ShapeDtypeStruct((s, LANES), F32),
            jax.ShapeDtypeStruct((s, LANES), jnp.int32),
            jax.ShapeDtypeStruct((1, e), F32),
        ],
        scratch_shapes=[pltpu.VMEM((1, e), F32)],
        compiler_params=_cparams(("arbitrary",)),
        name="router_topk",
    )(h, w_router, e_bias)


def _dest_kernel(idx_ref, rank_ref, ps_ref, dest_ref):
    idx = idx_ref[...]
    rank = rank_ref[...]
    ps = ps_ref[...]
    lane = lax.broadcasted_iota(jnp.int32, (idx.shape[0], ps.shape[1]), 1)
    out_lane = lax.broadcasted_iota(jnp.int32, idx.shape, 1)
    dest = jnp.zeros(idx.shape, jnp.int32)
    for k in range(TOP_K):
        start_k = jnp.sum(jnp.where(lane == idx[:, k:k + 1], ps, 0.0), axis=-1, keepdims=True)
        dest = jnp.where(out_lane == k, start_k.astype(jnp.int32) + rank, dest)
    dest_ref[...] = dest.T[:TOP_K, :]


def _dest(idx, rank, pstarts):
    s = idx.shape[0]
    e = pstarts.shape[1]
    tm = ROUTER_TM
    row = pl.BlockSpec((tm, LANES), lambda i: (i, 0))
    return pl.pallas_call(
        _dest_kernel,
        grid=(s // tm,),
        in_specs=[row, row, pl.BlockSpec((1, e), lambda i: (0, 0))],
        out_specs=pl.BlockSpec((TOP_K, tm), lambda i: (0, i)),
        out_shape=jax.ShapeDtypeStruct((TOP_K, s), jnp.int32),
        compiler_params=_cparams(("arbitrary",)),
        name="moe_dest",
    )(idx, rank, pstarts)


def _pack_rows(x):
    half = x.shape[1] // 2
    lo = lax.bitcast_convert_type(x[:, :half].astype(BF16).astype(F32), U32) >> 16
    hi = lax.bitcast_convert_type(x[:, half:].astype(BF16).astype(F32), U32) & HI_MASK
    return lo | hi


def _unpack_rows(w):
    lo = lax.bitcast_convert_type(w << 16, F32)
    hi = lax.bitcast_convert_type(w & HI_MASK, F32)
    return lo, hi


def _dispatch_kernel(dest_ref, h_ref, idx_ref, wt_ref, xs_ref, hp_ref, sem):
    tt = DISPATCH_TT
    dh = h_ref.shape[1] // 2
    hp_ref[:, :dh] = _pack_rows(h_ref[...])
    lane = lax.broadcasted_iota(I32, (tt, LANES), 1)
    tok = lax.broadcasted_iota(I32, (tt, LANES), 0) + pl.program_id(0) * tt
    experts = pltpu.roll(idx_ref[...], TAG_EXPERT0, 1)
    weights = pltpu.roll(lax.bitcast_convert_type(wt_ref[...], I32), TAG_EXPERT0 + TOP_K, 1)
    tag = jnp.where(lane == 0, tok,
                    jnp.where((lane >= TAG_EXPERT0) & (lane < TAG_EXPERT0 + TOP_K), experts,
                              jnp.where((lane >= TAG_EXPERT0 + TOP_K) & (lane < TAG_EXPERT0 + 2 * TOP_K), weights, 0)))
    hp_ref[:, dh:] = lax.bitcast_convert_type(tag, U32)

    def row_copy(t, k):
        return pltpu.make_async_copy(hp_ref.at[pl.ds(t, 1)], xs_ref.at[pl.ds(dest_ref[k, t], 1)], sem)

    def drain(t, carry):
        for k in range(TOP_K):
            row_copy(t, k).wait()
        return carry

    for t in range(tt):
        for k in range(TOP_K):
            row_copy(t, k).start()
    lax.fori_loop(0, tt, drain, 0)


def _dispatch(dest_t, h, idx, wt, n_rows):
    s, d = h.shape
    tt = DISPATCH_TT
    row = pl.BlockSpec((tt, LANES), lambda i: (i, 0))
    return pl.pallas_call(
        _dispatch_kernel,
        grid=(s // tt,),
        in_specs=[
            pl.BlockSpec((TOP_K, tt), lambda i: (0, i), memory_space=pltpu.SMEM),
            pl.BlockSpec((tt, d), lambda i: (i, 0)),
            row, row,
        ],
        out_specs=pl.BlockSpec(memory_space=pl.ANY),
        out_shape=jax.ShapeDtypeStruct((n_rows, d // 2 + LANES), U32),
        scratch_shapes=[pltpu.VMEM((tt, d // 2 + LANES), U32), pltpu.SemaphoreType.DMA(())],
        compiler_params=_cparams(("arbitrary",)),
        name="moe_dispatch",
    )(dest_t, h, idx, wt)


def _expert_kernel(start_ref, cnt_ref, xs_ref, wg_ref, wu_ref, wd_ref, y_ref,
                   wgb_ref, wub_ref, wdb_ref, xin_ref, yout_ref, slot_v_ref, slot_s_ref,
                   in_sem, out_sem, slot_sem, st_ref):
    e = pl.program_id(0)
    ne = pl.num_programs(0)
    rb, ch = MOE_RB, MOE_CH
    dh = yout_ref.shape[2]
    n_tok = (y_ref.shape[0] - 2 * rb) // TOP_K

    def n_chunks(cnt_e, j):
        return jnp.clip((cnt_e - j * rb + ch - 1) // ch, 0, rb // ch)

    def in_chunk(row0, slot, c):
        return pltpu.make_async_copy(xs_ref.at[pl.ds(pl.multiple_of(row0 + c * ch, ROW_ALIGN), ch)],
                                     xin_ref.at[slot, pl.ds(pl.multiple_of(c * ch, ch), ch)], in_sem.at[slot])

    def out_row(slot, r, dst):
        return pltpu.make_async_copy(yout_ref.at[slot, pl.ds(r, 1)], y_ref.at[pl.ds(dst, 1)], out_sem.at[slot])

    def scatter_start(n_groups, slot):
        for g in range(rb // SUBLANES):
            @pl.when(g < n_groups)
            def _():
                for u in range(SUBLANES):
                    r = g * SUBLANES + u
                    out_row(slot, r, slot_s_ref[r]).start()

    def scatter_wait(n_groups, slot):
        def body(g, carry):
            for u in range(SUBLANES):
                out_row(slot, 0, 0).wait()
            return carry
        lax.fori_loop(0, n_groups, body, 0)

    def for_chunks(n, fn):
        def body(c, carry):
            fn(c)
            return carry
        lax.fori_loop(0, n, body, 0)

    @pl.when(e == 0)
    def _():
        st_ref[0] = 0
        st_ref[1] = 0
        st_ref[2] = 0
        for_chunks(n_chunks(cnt_ref[0], 0), lambda c: in_chunk(start_ref[0], 0, c).start())

    wgb_ref[...] = wg_ref[0].astype(BF16)
    wub_ref[...] = wu_ref[0].astype(BF16)
    wdb_ref[...] = wd_ref[0].astype(BF16)

    start = start_ref[e]
    cnt = cnt_ref[e]
    n_sub = jnp.maximum(1, (cnt + rb - 1) // rb)
    e_next = jnp.minimum(e + 1, ne - 1)
    next_start = start_ref[e_next]
    next_first = jnp.where(e == ne - 1, 0, n_chunks(cnt_ref[e_next], 0))
    lane = lax.broadcasted_iota(I32, (rb, LANES), 1)

    def sub_block(j, carry):
        slot = st_ref[0]
        row0 = start + j * rb
        n_cur = n_chunks(cnt, j)
        n_live = jnp.clip(cnt - j * rb, 0, rb)
        last_j = j == n_sub - 1
        nxt_row = jnp.where(last_j, next_start, row0 + rb)
        n_nxt = jnp.where(last_j, next_first, n_chunks(cnt, j + 1))
        for_chunks(n_nxt, lambda c: in_chunk(nxt_row, 1 - slot, c).start())
        for_chunks(n_cur, lambda c: in_chunk(row0, slot, c).wait())

        tag = lax.bitcast_convert_type(xin_ref[slot, :, dh:], I32)
        mine = (tag == e) & (lane >= TAG_EXPERT0) & (lane < TAG_EXPERT0 + TOP_K)
        k_sel = jnp.sum(jnp.where(mine, lane - TAG_EXPERT0, 0).astype(F32), axis=-1, keepdims=True)
        w_sel = jnp.sum(jnp.where(pltpu.roll(jnp.where(mine, 1, 0), TOP_K, 1) > 0,
                                  lax.bitcast_convert_type(tag, F32), 0.0), axis=-1, keepdims=True)
        tok = jnp.sum(jnp.where(lane == 0, tag, 0).astype(F32), axis=-1, keepdims=True)
        row_i = lax.broadcasted_iota(I32, (rb, 1), 0)
        dst = jnp.where(row_i < n_live, (k_sel * n_tok + tok).astype(I32), TOP_K * n_tok + slot * rb + row_i)
        for g in range(rb // LANES):
            blk = jnp.broadcast_to(dst[g * LANES:(g + 1) * LANES], (LANES, LANES))
            slot_v_ref[:, g * LANES:(g + 1) * LANES] = blk.T[0:SUBLANES, :]
        slot_copy = pltpu.make_async_copy(slot_v_ref.at[0], slot_s_ref, slot_sem)
        slot_copy.start()

        xw = xin_ref[slot, :, :dh]
        live = (lax.broadcasted_iota(I32, xw.shape, 0) + j * rb) < cnt
        lo, hi = _unpack_rows(jnp.where(live, xw, jnp.zeros_like(xw)))
        xb = jnp.concatenate([lo.astype(BF16), hi.astype(BF16)], axis=1)
        gate = jnp.dot(xb, wgb_ref[...], preferred_element_type=F32)
        up = jnp.dot(xb, wub_ref[...], preferred_element_type=F32)
        hdn = (_silu(gate) * up).astype(BF16)
        y = jnp.dot(hdn, wdb_ref[...], preferred_element_type=F32) * w_sel

        scatter_wait(st_ref[1 + slot], slot)
        yout_ref[slot] = _pack_rows(y)
        slot_copy.wait()
        n_groups = (n_live + SUBLANES - 1) // SUBLANES
        scatter_start(n_groups, slot)
        st_ref[1 + slot] = n_groups
        st_ref[0] = 1 - slot
        return carry

    lax.fori_loop(0, n_sub, sub_block, 0)

    @pl.when(e == ne - 1)
    def _():
        scatter_wait(st_ref[1], 0)
        scatter_wait(st_ref[2], 1)


def _experts(starts, counts, xs, w_gate, w_up, w_down, n_tok):
    r, dw = xs.shape
    dh = dw - LANES
    ne, d, f = w_gate.shape
    rb = MOE_RB
    wsel = lambda e, st, ct: (e, 0, 0)
    grid_spec = pltpu.PrefetchScalarGridSpec(
        num_scalar_prefetch=2,
        grid=(ne,),
        in_specs=[
            pl.BlockSpec(memory_space=pl.ANY),
            pl.BlockSpec((1, d, f), wsel),
            pl.BlockSpec((1, d, f), wsel),
            pl.BlockSpec((1, f, d), wsel),
        ],
        out_specs=pl.BlockSpec(memory_space=pl.ANY),
        scratch_shapes=[
            pltpu.VMEM((d, f), BF16), pltpu.VMEM((d, f), BF16), pltpu.VMEM((f, d), BF16),
            pltpu.VMEM((2, rb, dw), U32), pltpu.VMEM((2, rb, dh), U32),
            pltpu.VMEM((SUBLANES, rb), I32), pltpu.SMEM((rb,), I32),
            pltpu.SemaphoreType.DMA((2,)), pltpu.SemaphoreType.DMA((2,)), pltpu.SemaphoreType.DMA(()),
            pltpu.SMEM((3,), I32),
        ],
    )
    return pl.pallas_call(
        _expert_kernel,
        grid_spec=grid_spec,
        out_shape=jax.ShapeDtypeStruct((TOP_K * n_tok + 2 * rb, dh), U32),
        compiler_params=_cparams(("arbitrary",)),
        name="moe_experts",
    )(starts, counts, xs, w_gate, w_up, w_down)


def _combine_kernel(*refs):
    y_refs = refs[:TOP_K]
    h_ref, wg_ref, wu_ref, wd_ref, g_ref, b_ref, o_ref, wgb_ref, wub_ref, wdb_ref = refs[TOP_K:]

    @pl.when(pl.program_id(0) == 0)
    def _():
        wgb_ref[...] = wg_ref[...].astype(BF16)
        wub_ref[...] = wu_ref[...].astype(BF16)
        wdb_ref[...] = wd_ref[...].astype(BF16)

    h = h_ref[...]
    hb = h.astype(BF16)
    gate = jnp.dot(hb, wgb_ref[...], preferred_element_type=F32)
    up = jnp.dot(hb, wub_ref[...], preferred_element_type=F32)
    hdn = (_silu(gate) * up).astype(BF16)
    shared = jnp.dot(hdn, wdb_ref[...], preferred_element_type=F32)

    r_lo, r_hi = _unpack_rows(y_refs[0][...])
    for k in range(1, TOP_K):
        lo, hi = _unpack_rows(y_refs[k][...])
        r_lo = r_lo + lo
        r_hi = r_hi + hi
    ff = jnp.concatenate([r_lo, r_hi], axis=1) + shared
    o_ref[...] = _layer_norm(ALPHA * h + ff, g_ref[...], b_ref[...])


def _combine(y, h, ws_gate, ws_up, ws_down, g, b):
    s, d = h.shape
    f = ws_gate.shape[1]
    tt = COMBINE_TT
    row = pl.BlockSpec((tt, d), lambda i: (i, 0))
    vec = pl.BlockSpec((1, d), lambda i: (0, 0))
    planes = [pl.BlockSpec((tt, d // 2), functools.partial(lambda k, i: (k * (s // tt) + i, 0), k))
              for k in range(TOP_K)]
    return pl.pallas_call(
        _combine_kernel,
        grid=(s // tt,),
        in_specs=planes + [
            row,
            pl.BlockSpec((d, f), lambda i: (0, 0)),
            pl.BlockSpec((d, f), lambda i: (0, 0)),
            pl.BlockSpec((f, d), lambda i: (0, 0)),
            vec, vec,
        ],
        out_specs=row,
        out_shape=jax.ShapeDtypeStruct((s, d), F32),
        scratch_shapes=[pltpu.VMEM((d, f), BF16), pltpu.VMEM((d, f), BF16), pltpu.VMEM((f, d), BF16)],
        compiler_params=_cparams(("arbitrary",)),
        name="moe_combine_ln2",
    )(*([y] * TOP_K), h, ws_gate, ws_up, ws_down, g, b)


def _rope_tables(positions):
    half = DA_QK_DIM // 2
    inv = ROPE_THETA ** (-jnp.arange(0, DA_QK_DIM, 2, dtype=F32) / DA_QK_DIM)
    ang = positions.astype(F32)[:, None] * inv
    cos = jnp.tile(jnp.cos(ang), (1, LANES // half))
    sin = jnp.sin(ang)
    sin = jnp.tile(jnp.concatenate([-sin, sin], axis=-1), (1, LANES // DA_QK_DIM))
    return cos, sin


def _moe(h, w_router, e_bias, w_gate, w_up, w_down, ws_gate, ws_up, ws_down, ln_g, ln_b):
    s, d = h.shape
    e = w_router.shape[1]
    idx, wt, rank, counts = _router(h, w_router, e_bias.reshape(1, e))
    counts = counts[0].astype(jnp.int32)
    acounts = (counts + ROW_ALIGN - 1) // ROW_ALIGN * ROW_ALIGN
    ends = jnp.cumsum(acounts)
    starts = (ends - acounts).astype(jnp.int32)
    dest_t = _dest(idx, rank, starts.astype(F32).reshape(1, e))
    n_rows = s * TOP_K + e * ROW_ALIGN + MOE_RB

    xs = _dispatch(dest_t, h, idx, wt, n_rows)
    y = _experts(starts, counts, xs, w_gate, w_up, w_down, s)
    return _combine(y, h, ws_gate, ws_up, ws_down, ln_g.reshape(1, d), ln_b.reshape(1, d))


def kernel(x, positions, ln_in_g, ln_in_b, w_in, w_out, lam_q1, lam_k1, lam_q2, lam_k2, da_norm_g,
           hg_lb_logits, hg_norm_g, ln1_g, ln1_b, w_router, e_bias, w_gate, w_up, w_down,
           ws_gate, ws_up, ws_down, ln2_g, ln2_b):
    bsz, s, d = x.shape
    assert bsz == 1 and w_in.shape[0] == DEPTH
    x2 = x.reshape(s, d)
    cos, sin = _rope_tables(positions.reshape(s))
    lower_bounds = jnp.cumsum(jax.nn.softmax(hg_lb_logits.astype(F32), axis=0), axis=0)
    lam = (jnp.exp(jnp.sum(lam_q1[0] * lam_k1[0])) - jnp.exp(jnp.sum(lam_q2[0] * lam_k2[0]))
           + LAMBDA_INIT).reshape(1).astype(F32)
    gi, bi = ln_in_g.reshape(1, d), ln_in_b.reshape(1, d)

    da_w = DA_HEADS * LANES
    hb = _ln_in(x2, gi, bi)
    w_in_b = w_in[0].astype(BF16)
    w_out_b = w_out[0].astype(BF16)
    qm = _proj("q", hb, w_in_b, 0, da_w, cos, sin)
    kr = _proj("k", hb, w_in_b, da_w, da_w, cos, sin)
    vb = _proj("v", hb, w_in_b, 2 * da_w, da_w)
    proj_h = _proj("plain", hb, w_in_b, 3 * da_w, w_in.shape[2] - 3 * da_w)
    y_a = _attention(lam, qm, kr, vb, da_norm_g[0].reshape(1, LANES))
    y_b = _hgrn(proj_h, lower_bounds[0].reshape(1, -1), hg_norm_g[0].reshape(1, LANES), 0)
    h1 = _outproj(y_a, y_b, w_out_b, x2, gi, bi, ln1_g[0].reshape(1, d), ln1_b[0].reshape(1, d))
    h2 = _moe(h1, w_router[0], e_bias[0], w_gate[0], w_up[0], w_down[0],
              ws_gate[0], ws_up[0], ws_down[0], ln2_g[0], ln2_b[0])
    return h2.reshape(bsz, s, d)
```

```python
import functools
import math

import jax
import jax.numpy as jnp
import numpy as np
from jax import lax
from jax.experimental import pallas as pl
from jax.experimental.pallas import tpu as pltpu

F32 = jnp.float32
BF16 = jnp.bfloat16
U32 = jnp.uint32
I32 = jnp.int32
HI_MASK = np.uint32(0xFFFF0000)

CHUNK = 64
DA_HEADS = 8
DA_QK_DIM = 64
DA_V_DIM = 2 * DA_QK_DIM
HG_HEADS = 8
HG_DIM = 128
ROPE_THETA = 10000.0
N_EXPERTS = 256
TOP_K = 8
N_GROUPS = 8
TOPK_GROUPS = 4
GROUP_SIZE = N_EXPERTS // N_GROUPS
ROUTED_SCALE = 2.5
DEPTH = 1
ALPHA = (2.0 * DEPTH) ** 0.25
LN_EPS = 1e-5
RMS_EPS = 1e-5
LAMBDA_INIT = 0.8 - 0.6 * math.exp(-0.3 * 0)

LANES = 128
VMEM_LIMIT = 56 * 1024 * 1024
NEG = -1e30

LN_TM = 512
INPROJ_TM, INPROJ_TN = 1024, 512
ATTN_T = 1024
HG_TR = 512
HG_SUB = 16
HG_GRP = 128
OUT_TM, OUT_TK = 512, 512
ROUTER_TM = 512
MOE_RB = 256
MOE_CH = 64
SUBLANES = 8
ROW_ALIGN = SUBLANES
TAG_EXPERT0 = 8
DISPATCH_TT = 256
COMBINE_TT = 128


def _cparams(sem):
    return pltpu.CompilerParams(dimension_semantics=sem, vmem_limit_bytes=VMEM_LIMIT)


def _layer_norm(xf, g, b):
    mu = jnp.mean(xf, axis=-1, keepdims=True)
    xc = xf - mu
    var = jnp.mean(xc * xc, axis=-1, keepdims=True)
    return xc * lax.rsqrt(var + LN_EPS) * g + b


def _silu(x):
    return x * jax.nn.sigmoid(x)


def _ln_in_kernel(x_ref, g_ref, b_ref, o_ref):
    o_ref[...] = _layer_norm(x_ref[...], g_ref[...], b_ref[...]).astype(o_ref.dtype)


def _ln_in(x, g, b):
    s, d = x.shape
    tm = LN_TM
    vec = pl.BlockSpec((1, d), lambda i: (0, 0))
    return pl.pallas_call(
        _ln_in_kernel,
        grid=(s // tm,),
        in_specs=[pl.BlockSpec((tm, d), lambda i: (i, 0)), vec, vec],
        out_specs=pl.BlockSpec((tm, d), lambda i: (i, 0)),
        out_shape=jax.ShapeDtypeStruct((s, d), BF16),
        compiler_params=_cparams(("arbitrary",)),
        name="ln_in",
    )(x, g, b)


def _rope(x, cos, sin, first_half):
    rot = jnp.where(first_half, pltpu.roll(x, LANES - DA_QK_DIM // 2, 1), pltpu.roll(x, DA_QK_DIM // 2, 1))
    return x * cos + rot * sin


def _proj_kernel(role, h_ref, w_ref, *rest):
    acc = jnp.dot(h_ref[...], w_ref[...], preferred_element_type=F32)
    if role == "plain":
        (o_ref,) = rest
        o_ref[...] = acc
        return
    heads = acc.shape[1] // LANES
    if role == "v":
        (o_ref,) = rest
        for hh in range(heads):
            o_ref[hh] = acc[:, hh * LANES:(hh + 1) * LANES].astype(BF16)
        return
    cos_ref, sin_ref, o_ref = rest
    cos = cos_ref[...]
    sin = sin_ref[...]
    lane = lax.broadcasted_iota(jnp.int32, cos.shape, 1)
    first_half = (lane % DA_QK_DIM) < (DA_QK_DIM // 2)
    map0 = lane < DA_QK_DIM
    for hh in range(heads):
        r = _rope(acc[:, hh * LANES:(hh + 1) * LANES], cos, sin, first_half)
        if role == "k":
            o_ref[hh] = r.astype(BF16)
        else:
            r = r * (DA_QK_DIM ** -0.5)
            o_ref[hh, 0] = jnp.where(map0, r, 0.0).astype(BF16)
            o_ref[hh, 1] = jnp.where(map0, 0.0, r).astype(BF16)


def _proj(role, hb, w, col0, ncols, cos=None, sin=None):
    s, d = hb.shape
    tm, tn = INPROJ_TM, INPROJ_TN
    jb = col0 // tn
    hpt = tn // LANES
    nh = ncols // LANES
    in_specs = [pl.BlockSpec((tm, d), lambda i, j: (i, 0)), pl.BlockSpec((d, tn), lambda i, j: (0, jb + j))]
    args = [hb, w]
    if role in ("q", "k"):
        tab = pl.BlockSpec((tm, LANES), lambda i, j: (i, 0))
        in_specs += [tab, tab]
        args += [cos, sin]
    if role == "plain":
        out_spec = pl.BlockSpec((tm, tn), lambda i, j: (i, j))
        out_shape = jax.ShapeDtypeStruct((s, ncols), F32)
    elif role == "q":
        out_spec = pl.BlockSpec((hpt, 2, tm, LANES), lambda i, j: (j, 0, i, 0))
        out_shape = jax.ShapeDtypeStruct((nh, 2, s, LANES), BF16)
    else:
        out_spec = pl.BlockSpec((hpt, tm, LANES), lambda i, j: (j, i, 0))
        out_shape = jax.ShapeDtypeStruct((nh, s, LANES), BF16)
    return pl.pallas_call(
        functools.partial(_proj_kernel, role),
        grid=(s // tm, ncols // tn),
        in_specs=in_specs,
        out_specs=out_spec,
        out_shape=out_shape,
        compiler_params=_cparams(("arbitrary", "arbitrary")),
        name="inproj_" + role,
    )(*args)


def _attn_kernel(lam_ref, q_ref, k_ref, v_ref, g_ref, o_ref, acc_ref, m_ref, l_ref):
    t = ATTN_T
    i = pl.program_id(1)
    hf = t // 2
    qa = q_ref[0]
    q = jnp.concatenate([qa[0, :hf], qa[1, :hf], qa[0, hf:], qa[1, hf:]], axis=0)
    m_ref[...] = jnp.full(m_ref.shape, NEG, F32)
    l_ref[...] = jnp.zeros(l_ref.shape, F32)
    acc_ref[...] = jnp.zeros(acc_ref.shape, F32)

    def step(start, width, rows, masked):
        k = k_ref[0, pl.ds(start, width), :]
        v = v_ref[0, pl.ds(start, width), :]
        s = lax.dot_general(q[rows], k, (((1,), (1,)), ((), ())), preferred_element_type=F32)
        if masked:
            row = lax.broadcasted_iota(jnp.int32, s.shape, 0) % hf
            col = lax.broadcasted_iota(jnp.int32, s.shape, 1)
            s = jnp.where((col // CHUNK) <= (row // CHUNK), s, NEG)
        cols = [s[:, j * LANES:(j + 1) * LANES] for j in range(width // LANES)]
        mx = cols[0]
        for cj in cols[1:]:
            mx = jnp.maximum(mx, cj)
        m_prev = m_ref[rows, :]
        m_new = jnp.maximum(m_prev, jnp.max(mx, axis=-1, keepdims=True))
        alpha = jnp.exp(m_prev - m_new)
        ps = [jnp.exp(cj - m_new) for cj in cols]
        lsum = ps[0]
        for pj in ps[1:]:
            lsum = lsum + pj
        l_ref[rows, :] = alpha * l_ref[rows, :] + lsum
        m_ref[rows, :] = m_new
        p = jnp.concatenate([pj.astype(BF16) for pj in ps], axis=-1)
        acc_ref[rows, :] = alpha * acc_ref[rows, :] + jnp.dot(p, v, preferred_element_type=F32)

    def body(c, carry):
        step(pl.multiple_of(c * t, t), t, slice(0, 2 * t), False)
        return carry

    lax.fori_loop(0, i, body, 0)
    d0 = pl.multiple_of(i * t, t)
    step(d0, hf, slice(0, t), True)
    step(d0, hf, slice(t, 2 * t), False)
    step(pl.multiple_of(d0 + hf, hf), hf, slice(t, 2 * t), True)

    lam = lam_ref[0]
    acc = acc_ref[...]
    l = jnp.sum(l_ref[...], axis=-1, keepdims=True)
    o1 = jnp.concatenate([acc[:hf] / l[:hf], acc[t:t + hf] / l[t:t + hf]], axis=0)
    o2 = jnp.concatenate([acc[hf:t] / l[hf:t], acc[t + hf:] / l[t + hf:]], axis=0)
    o = o1 - lam * o2
    o = o * lax.rsqrt(jnp.mean(o * o, axis=-1, keepdims=True) + RMS_EPS) * g_ref[...] * (1.0 - LAMBDA_INIT)
    o_ref[...] = o.astype(o_ref.dtype)


def _attention(lam, qm, kr, vb, norm_g):
    h, _, s, _ = qm.shape
    t = ATTN_T
    return pl.pallas_call(
        _attn_kernel,
        grid=(h, s // t),
        in_specs=[
            pl.BlockSpec(memory_space=pltpu.SMEM),
            pl.BlockSpec((1, 2, t, LANES), lambda hh, i: (hh, 0, i, 0)),
            pl.BlockSpec((1, s, LANES), lambda hh, i: (hh, 0, 0)),
            pl.BlockSpec((1, s, LANES), lambda hh, i: (hh, 0, 0)),
            pl.BlockSpec((1, LANES), lambda hh, i: (0, 0)),
        ],
        out_specs=pl.BlockSpec((t, LANES), lambda hh, i: (i, hh)),
        out_shape=jax.ShapeDtypeStruct((s, h * LANES), BF16),
        scratch_shapes=[
            pltpu.VMEM((2 * t, LANES), F32),
            pltpu.VMEM((2 * t, LANES), F32),
            pltpu.VMEM((2 * t, LANES), F32),
        ],
        compiler_params=_cparams(("arbitrary", "arbitrary")),
        name="diff_attn",
    )(lam, qm, kr, vb, norm_g)


def _hgrn_kernel(q_ref, f_ref, i_ref, g_ref, lb_ref, ng_ref, o_ref,
                 st_ref, qs_ref, kk_ref, b_ref, kh_ref, od_ref, iv_ref):
    tr, sub, grp = HG_TR, HG_SUB, HG_GRP

    @pl.when(pl.program_id(1) == 0)
    def _():
        st_ref[...] = jnp.zeros(st_ref.shape, F32)

    lb = lb_ref[...]
    f = lb + (1.0 - lb) * jax.nn.sigmoid(f_ref[...])
    log_f = jnp.log(f)
    kk = 1.0 - f
    qs_ref[...] = _silu(q_ref[...])
    zpad = jnp.zeros((sub, LANES), F32)
    kk_ref[0:sub, :] = zpad
    b_ref[0:sub, :] = zpad
    iv_ref[0:sub, :] = zpad
    kk_ref[sub:, :] = kk
    iv_ref[sub:, :] = i_ref[...]

    r = lax.broadcasted_iota(jnp.int32, (grp, grp), 0)
    c = lax.broadcasted_iota(jnp.int32, (grp, grp), 1)
    same = (r // sub) == (c // sub)
    tri = jnp.where(same & (c <= r), 1.0, 0.0).astype(F32)
    blk = jnp.where(same, 1.0, 0.0).astype(F32)
    grp_iota = lax.broadcasted_iota(jnp.int32, (grp, LANES), 0)
    pos_in_sub = grp_iota % sub

    for gi in range(tr // grp):
        g0 = gi * grp
        rows = slice(g0, g0 + grp)
        prow = slice(sub + g0, sub + g0 + grp)
        lf = log_f[rows]
        bg = jnp.dot(tri, lf, preferred_element_type=F32, precision=lax.Precision.HIGHEST)
        tot = jnp.dot(blk, lf, preferred_element_type=F32, precision=lax.Precision.HIGHEST)
        b_ref[prow, :] = bg
        kh_ref[rows, :] = kk[rows] * jnp.exp(tot - bg)
        qg = qs_ref[rows, :]
        acc = jnp.sum(qg * kk[rows], axis=-1, keepdims=True) * i_ref[rows, :]
        for lag in range(1, sub):
            srow = slice(sub + g0 - lag, sub + g0 - lag + grp)
            e = jnp.exp(jnp.where(pos_in_sub >= lag, bg - b_ref[srow, :], NEG))
            w = jnp.sum(qg * kk_ref[srow, :] * e, axis=-1, keepdims=True)
            acc = acc + w * iv_ref[srow, :]
        od_ref[rows, :] = acc

    st = st_ref[...]
    for gi in range(tr // grp):
        g0 = gi * grp
        iv_t = i_ref[g0:g0 + grp, :].T.astype(BF16)
        kh_g = kh_ref[g0:g0 + grp, :]
        upds = []
        for j in range(grp // sub):
            in_sub = (grp_iota >= j * sub) & (grp_iota < (j + 1) * sub)
            upds.append(jnp.dot(iv_t, jnp.where(in_sub, kh_g, 0.0).astype(BF16), preferred_element_type=F32))
        for j in range(grp // sub):
            base = g0 + j * sub
            bj = b_ref[sub + base:sub + base + sub, :]
            qj = qs_ref[base:base + sub, :]
            o_inter = lax.dot_general((qj * jnp.exp(bj)).astype(BF16), st.astype(BF16),
                                      (((1,), (1,)), ((), ())), preferred_element_type=F32)
            od_ref[base:base + sub, :] = od_ref[base:base + sub, :] + o_inter
            st = st * jnp.exp(bj[sub - 1:sub]) + upds[j]
    st_ref[...] = st

    o = od_ref[...]
    o = o * lax.rsqrt(jnp.mean(o * o, axis=-1, keepdims=True) + RMS_EPS) * ng_ref[...]
    o_ref[...] = (o * _silu(g_ref[...])).astype(o_ref.dtype)


def _hgrn(proj, lower_bound, norm_g, col0):
    s = proj.shape[0]
    tr, h = HG_TR, HG_HEADS
    blk = lambda off: pl.BlockSpec((tr, LANES), lambda hh, i: (i, col0 + off + hh))
    pad = pltpu.VMEM((tr + HG_SUB, LANES), F32)
    full = pltpu.VMEM((tr, LANES), F32)
    return pl.pallas_call(
        _hgrn_kernel,
        grid=(h, s // tr),
        in_specs=[blk(0), blk(h), blk(2 * h), blk(3 * h),
                  pl.BlockSpec((1, LANES), lambda hh, i: (0, hh)),
                  pl.BlockSpec((1, LANES), lambda hh, i: (0, 0))],
        out_specs=pl.BlockSpec((tr, LANES), lambda hh, i: (i, hh)),
        out_shape=jax.ShapeDtypeStruct((s, h * LANES), BF16),
        scratch_shapes=[pltpu.VMEM((HG_DIM, HG_DIM), F32), full, pad, pad, full, full, pad],
        compiler_params=_cparams(("arbitrary", "arbitrary")),
        name="hgrn2",
    )(proj, proj, proj, proj, lower_bound, norm_g)


def _outproj_kernel(ya_ref, yb_ref, w_ref, x_ref, gi_ref, bi_ref, g1_ref, b1_ref, h_ref, acc_ref):
    k = pl.program_id(1)
    nk = pl.num_programs(1)

    @pl.when(k == 0)
    def _():
        acc_ref[...] = jnp.zeros(acc_ref.shape, F32)

    w = w_ref[...]

    @pl.when(k < nk // 2)
    def _():
        acc_ref[...] += jnp.dot(ya_ref[...], w, preferred_element_type=F32)

    @pl.when(k >= nk // 2)
    def _():
        acc_ref[...] += jnp.dot(yb_ref[...], w, preferred_element_type=F32)

    @pl.when(k == nk - 1)
    def _():
        h0 = _layer_norm(x_ref[...], gi_ref[...], bi_ref[...])
        h_ref[...] = _layer_norm(ALPHA * h0 + acc_ref[...], g1_ref[...], b1_ref[...])


def _outproj(ya, yb, w, x, gi, bi, g1, b1):
    s, d = x.shape
    tm, tk = OUT_TM, OUT_TK
    nka = ya.shape[1] // tk
    nk = w.shape[0] // tk
    assert nk == 2 * nka and yb.shape[1] == ya.shape[1]
    vec = pl.BlockSpec((1, d), lambda i, k: (0, 0))
    return pl.pallas_call(
        _outproj_kernel,
        grid=(s // tm, nk),
        in_specs=[
            pl.BlockSpec((tm, tk), lambda i, k: (i, jnp.minimum(k, nka - 1))),
            pl.BlockSpec((tm, tk), lambda i, k: (i, jnp.maximum(k - nka, 0))),
            pl.BlockSpec((tk, d), lambda i, k: (k, 0)),
            pl.BlockSpec((tm, d), lambda i, k: (i, 0)),
            vec, vec, vec, vec,
        ],
        out_specs=pl.BlockSpec((tm, d), lambda i, k: (i, 0)),
        out_shape=jax.ShapeDtypeStruct((s, d), F32),
        scratch_shapes=[pltpu.VMEM((tm, d), F32)],
        compiler_params=_cparams(("arbitrary", "arbitrary")),
        name="outproj_ln1",
    )(ya, yb, w, x, gi, bi, g1, b1)


def _router_kernel(h_ref, w_ref, eb_ref, idx_ref, wt_ref, rank_ref, cnt_ref, carry_ref):
    tm = ROUTER_TM
    i = pl.program_id(0)

    @pl.when(i == 0)
    def _():
        carry_ref[...] = jnp.zeros(carry_ref.shape, F32)

    logits = jnp.dot(h_ref[...], w_ref[...], preferred_element_type=F32, precision=lax.Precision.HIGHEST)
    scores = jax.nn.sigmoid(logits)
    choice = scores + eb_ref[...]
    lane = lax.broadcasted_iota(jnp.int32, choice.shape, 1)
    grp_of_lane = lane // GROUP_SIZE
    big = jnp.int32(N_EXPERTS)
    ninf = -jnp.inf

    def first_argmax(vals):
        m = jnp.max(vals, axis=-1, keepdims=True)
        idx = jnp.min(jnp.where(vals == m, lane, big), axis=-1, keepdims=True)
        return m, idx

    gs = []
    for g in range(N_GROUPS):
        vals = jnp.where(grp_of_lane == g, choice, ninf)
        m1, i1 = first_argmax(vals)
        m2 = jnp.max(jnp.where(lane == i1, ninf, vals), axis=-1, keepdims=True)
        gs.append(m1 + m2)
    allowed = jnp.zeros(choice.shape, jnp.int32)
    for g in range(N_GROUPS):
        beats = jnp.zeros(gs[g].shape, jnp.int32)
        for g2 in range(N_GROUPS):
            if g2 == g:
                continue
            better = (gs[g2] >= gs[g]) if g2 < g else (gs[g2] > gs[g])
            beats = beats + jnp.where(better, 1, 0)
        allowed = jnp.where(grp_of_lane == g, jnp.where(beats < TOPK_GROUPS, 1, 0), allowed)
    masked = jnp.where(allowed > 0, choice, ninf)

    sel = jnp.zeros(choice.shape, F32)
    idxs, wts = [], []
    for _ in range(TOP_K):
        _, ik = first_argmax(masked)
        hit = lane == ik
        idxs.append(ik)
        wts.append(jnp.sum(jnp.where(hit, scores, 0.0), axis=-1, keepdims=True))
        sel = jnp.where(hit, 1.0, sel)
        masked = jnp.where(hit, ninf, masked)
    wsum = wts[0]
    for k in range(1, TOP_K):
        wsum = wsum + wts[k]

    r = lax.broadcasted_iota(jnp.int32, (tm, tm), 0)
    c = lax.broadcasted_iota(jnp.int32, (tm, tm), 1)
    strict_lower = jnp.where(c < r, 1.0, 0.0).astype(BF16)
    rank = jnp.dot(strict_lower, sel.astype(BF16), preferred_element_type=F32) + carry_ref[...]
    carry_ref[...] = carry_ref[...] + jnp.sum(sel, axis=0, keepdims=True)
    cnt_ref[...] = carry_ref[...]

    out_lane = lax.broadcasted_iota(jnp.int32, (tm, LANES), 1)
    idx_out = jnp.zeros((tm, LANES), jnp.int32)
    wt_out = jnp.zeros((tm, LANES), F32)
    rank_out = jnp.zeros((tm, LANES), jnp.int32)
    for k in range(TOP_K):
        rk = jnp.sum(jnp.where(lane == idxs[k], rank, 0.0), axis=-1, keepdims=True)
        idx_out = jnp.where(out_lane == k, idxs[k], idx_out)
        wt_out = jnp.where(out_lane == k, wts[k] / wsum * ROUTED_SCALE, wt_out)
        rank_out = jnp.where(out_lane == k, rk.astype(jnp.int32), rank_out)
    idx_ref[...] = idx_out
    wt_ref[...] = wt_out
    rank_ref[...] = rank_out


def _router(h, w_router, e_bias):
    s, d = h.shape
    e = w_router.shape[1]
    tm = ROUTER_TM
    row = pl.BlockSpec((tm, LANES), lambda i: (i, 0))
    return pl.pallas_call(
        _router_kernel,
        grid=(s // tm,),
        in_specs=[
            pl.BlockSpec((tm, d), lambda i: (i, 0)),
            pl.BlockSpec((d, e), lambda i: (0, 0)),
            pl.BlockSpec((1, e), lambda i: (0, 0)),
        ],
        out_specs=[row, row, row, pl.BlockSpec((1, e), lambda i: (0, 0))],
        out_shape=[
            jax.ShapeDtypeStruct((s, LANES), jnp.int32),
            jax.ShapeDtypeStruct((s, LANES), F32),
            jax.ShapeDtypeStruct((s, LANES), jnp.int32),
            jax.ShapeDtypeStruct((1, e), F32),
        ],
        scratch_shapes=[pltpu.VMEM((1, e), F32)],
        compiler_params=_cparams(("arbitrary",)),
        name="router_topk",
    )(h, w_router, e_bias)


def _dest_kernel(idx_ref, rank_ref, ps_ref, dest_ref):
    idx = idx_ref[...]
    rank = rank_ref[...]
    ps = ps_ref[...]
    lane = lax.broadcasted_iota(jnp.int32, (idx.shape[0], ps.shape[1]), 1)
    out_lane = lax.broadcasted_iota(jnp.int32, idx.shape, 1)
    dest = jnp.zeros(idx.shape, jnp.int32)
    for k in range(TOP_K):
        start_k = jnp.sum(jnp.where(lane == idx[:, k:k + 1], ps, 0.0), axis=-1, keepdims=True)
        dest = jnp.where(out_lane == k, start_k.astype(jnp.int32) + rank, dest)
    dest_ref[...] = dest.T[:TOP_K, :]


def _dest(idx, rank, pstarts):
    s = idx.shape[0]
    e = pstarts.shape[1]
    tm = ROUTER_TM
    row = pl.BlockSpec((tm, LANES), lambda i: (i, 0))
    return pl.pallas_call(
        _dest_kernel,
        grid=(s // tm,),
        in_specs=[row, row, pl.BlockSpec((1, e), lambda i: (0, 0))],
        out_specs=pl.BlockSpec((TOP_K, tm), lambda i: (0, i)),
        out_shape=jax.ShapeDtypeStruct((TOP_K, s), jnp.int32),
        compiler_params=_cparams(("arbitrary",)),
        name="moe_dest",
    )(idx, rank, pstarts)


def _pack_rows(x):
    half = x.shape[1] // 2
    lo = lax.bitcast_convert_type(x[:, :half].astype(BF16).astype(F32), U32) >> 16
    hi = lax.bitcast_convert_type(x[:, half:].astype(BF16).astype(F32), U32) & HI_MASK
    return lo | hi


def _unpack_rows(w):
    lo = lax.bitcast_convert_type(w << 16, F32)
    hi = lax.bitcast_convert_type(w & HI_MASK, F32)
    return lo, hi


def _dispatch_kernel(dest_ref, h_ref, idx_ref, wt_ref, xs_ref, hp_ref, sem):
    tt = DISPATCH_TT
    dh = h_ref.shape[1] // 2
    hp_ref[:, :dh] = _pack_rows(h_ref[...])
    lane = lax.broadcasted_iota(I32, (tt, LANES), 1)
    tok = lax.broadcasted_iota(I32, (tt, LANES), 0) + pl.program_id(0) * tt
    experts = pltpu.roll(idx_ref[...], TAG_EXPERT0, 1)
    weights = pltpu.roll(lax.bitcast_convert_type(wt_ref[...], I32), TAG_EXPERT0 + TOP_K, 1)
    tag = jnp.where(lane == 0, tok,
                    jnp.where((lane >= TAG_EXPERT0) & (lane < TAG_EXPERT0 + TOP_K), experts,
                              jnp.where((lane >= TAG_EXPERT0 + TOP_K) & (lane < TAG_EXPERT0 + 2 * TOP_K), weights, 0)))
    hp_ref[:, dh:] = lax.bitcast_convert_type(tag, U32)

    def row_copy(t, k):
        return pltpu.make_async_copy(hp_ref.at[pl.ds(t, 1)], xs_ref.at[pl.ds(dest_ref[k, t], 1)], sem)

    def drain(t, carry):
        for k in range(TOP_K):
            row_copy(t, k).wait()
        return carry

    for t in range(tt):
        for k in range(TOP_K):
            row_copy(t, k).start()
    lax.fori_loop(0, tt, drain, 0)


def _dispatch(dest_t, h, idx, wt, n_rows):
    s, d = h.shape
    tt = DISPATCH_TT
    row = pl.BlockSpec((tt, LANES), lambda i: (i, 0))
    return pl.pallas_call(
        _dispatch_kernel,
        grid=(s // tt,),
        in_specs=[
            pl.BlockSpec((TOP_K, tt), lambda i: (0, i), memory_space=pltpu.SMEM),
            pl.BlockSpec((tt, d), lambda i: (i, 0)),
            row, row,
        ],
        out_specs=pl.BlockSpec(memory_space=pl.ANY),
        out_shape=jax.ShapeDtypeStruct((n_rows, d // 2 + LANES), U32),
        scratch_shapes=[pltpu.VMEM((tt, d // 2 + LANES), U32), pltpu.SemaphoreType.DMA(())],
        compiler_params=_cparams(("arbitrary",)),
        name="moe_dispatch",
    )(dest_t, h, idx, wt)


def _expert_kernel(start_ref, cnt_ref, xs_ref, wg_ref, wu_ref, wd_ref, y_ref,
                   wgb_ref, wub_ref, wdb_ref, xin_ref, yout_ref, slot_v_ref, slot_s_ref,
                   in_sem, out_sem, slot_sem, st_ref):
    e = pl.program_id(0)
    ne = pl.num_programs(0)
    rb, ch = MOE_RB, MOE_CH
    dh = yout_ref.shape[2]
    n_tok = (y_ref.shape[0] - 2 * rb) // TOP_K

    def n_chunks(cnt_e, j):
        return jnp.clip((cnt_e - j * rb + ch - 1) // ch, 0, rb // ch)

    def in_chunk(row0, slot, c):
        return pltpu.make_async_copy(xs_ref.at[pl.ds(pl.multiple_of(row0 + c * ch, ROW_ALIGN), ch)],
                                     xin_ref.at[slot, pl.ds(pl.multiple_of(c * ch, ch), ch)], in_sem.at[slot])

    def out_row(slot, r, dst):
        return pltpu.make_async_copy(yout_ref.at[slot, pl.ds(r, 1)], y_ref.at[pl.ds(dst, 1)], out_sem.at[slot])

    def scatter_start(n_groups, slot):
        for g in range(rb // SUBLANES):
            @pl.when(g < n_groups)
            def _():
                for u in range(SUBLANES):
                    r = g * SUBLANES + u
                    out_row(slot, r, slot_s_ref[r]).start()

    def scatter_wait(n_groups, slot):
        def body(g, carry):
            for u in range(SUBLANES):
                out_row(slot, 0, 0).wait()
            return carry
        lax.fori_loop(0, n_groups, body, 0)

    def for_chunks(n, fn):
        def body(c, carry):
            fn(c)
            return carry
        lax.fori_loop(0, n, body, 0)

    @pl.when(e == 0)
    def _():
        st_ref[0] = 0
        st_ref[1] = 0
        st_ref[2] = 0
        for_chunks(n_chunks(cnt_ref[0], 0), lambda c: in_chunk(start_ref[0], 0, c).start())

    start = start_ref[e]
    cnt = cnt_ref[e]
    n_sub = jnp.maximum(1, (cnt + rb - 1) // rb)
    e_next = jnp.minimum(e + 1, ne - 1)
    next_start = start_ref[e_next]
    next_first = jnp.where(e == ne - 1, 0, n_chunks(cnt_ref[e_next], 0))
    lane = lax.broadcasted_iota(I32, (rb, LANES), 1)

    def sub_block(j, carry, first=False):
        slot = st_ref[0]
        row0 = start + j * rb
        n_cur = n_chunks(cnt, j)
        n_live = jnp.clip(cnt - j * rb, 0, rb)
        last_j = j == n_sub - 1
        nxt_row = jnp.where(last_j, next_start, row0 + rb)
        n_nxt = jnp.where(last_j, next_first, n_chunks(cnt, j + 1))
        for_chunks(n_nxt, lambda c: in_chunk(nxt_row, 1 - slot, c).start())
        for_chunks(n_cur, lambda c: in_chunk(row0, slot, c).wait())

        tag = lax.bitcast_convert_type(xin_ref[slot, :, dh:], I32)
        mine = (tag == e) & (lane >= TAG_EXPERT0) & (lane < TAG_EXPERT0 + TOP_K)
        k_sel = jnp.sum(jnp.where(mine, lane - TAG_EXPERT0, 0).astype(F32), axis=-1, keepdims=True)
        w_sel = jnp.sum(jnp.where(pltpu.roll(jnp.where(mine, 1, 0), TOP_K, 1) > 0,
                                  lax.bitcast_convert_type(tag, F32), 0.0), axis=-1, keepdims=True)
        tok = jnp.sum(jnp.where(lane == 0, tag, 0).astype(F32), axis=-1, keepdims=True)
        row_i = lax.broadcasted_iota(I32, (rb, 1), 0)
        dst = jnp.where(row_i < n_live, (k_sel * n_tok + tok).astype(I32), TOP_K * n_tok + slot * rb + row_i)
        for g in range(rb // LANES):
            blk = jnp.broadcast_to(dst[g * LANES:(g + 1) * LANES], (LANES, LANES))
            slot_v_ref[:, g * LANES:(g + 1) * LANES] = blk.T[0:SUBLANES, :]
        slot_copy = pltpu.make_async_copy(slot_v_ref.at[0], slot_s_ref, slot_sem)
        slot_copy.start()

        xw = xin_ref[slot, :, :dh]
        live = (lax.broadcasted_iota(I32, xw.shape, 0) + j * rb) < cnt
        lo, hi = _unpack_rows(jnp.where(live, xw, jnp.zeros_like(xw)))
        xb = jnp.concatenate([lo.astype(BF16), hi.astype(BF16)], axis=1)
        if first:
            wgb_ref[...] = wg_ref[0].astype(BF16)
        gate = jnp.dot(xb, wgb_ref[...], preferred_element_type=F32)
        if first:
            wub_ref[...] = wu_ref[0].astype(BF16)
        up = jnp.dot(xb, wub_ref[...], preferred_element_type=F32)
        if first:
            wdb_ref[...] = wd_ref[0].astype(BF16)
        hdn = (_silu(gate) * up).astype(BF16)
        y = jnp.dot(hdn, wdb_ref[...], preferred_element_type=F32) * w_sel

        scatter_wait(st_ref[1 + slot], slot)
        yout_ref[slot] = _pack_rows(y)
        slot_copy.wait()
        n_groups = (n_live + SUBLANES - 1) // SUBLANES
        scatter_start(n_groups, slot)
        st_ref[1 + slot] = n_groups
        st_ref[0] = 1 - slot
        return carry

    sub_block(0, 0, first=True)
    lax.fori_loop(1, n_sub, sub_block, 0)

    @pl.when(e == ne - 1)
    def _():
        scatter_wait(st_ref[1], 0)
        scatter_wait(st_ref[2], 1)


def _experts(starts, counts, xs, w_gate, w_up, w_down, n_tok):
    r, dw = xs.shape
    dh = dw - LANES
    ne, d, f = w_gate.shape
    rb = MOE_RB
    wsel = lambda e, st, ct: (e, 0, 0)
    grid_spec = pltpu.PrefetchScalarGridSpec(
        num_scalar_prefetch=2,
        grid=(ne,),
        in_specs=[
            pl.BlockSpec(memory_space=pl.ANY),
            pl.BlockSpec((1, d, f), wsel),
            pl.BlockSpec((1, d, f), wsel),
            pl.BlockSpec((1, f, d), wsel),
        ],
        out_specs=pl.BlockSpec(memory_space=pl.ANY),
        scratch_shapes=[
            pltpu.VMEM((d, f), BF16), pltpu.VMEM((d, f), BF16), pltpu.VMEM((f, d), BF16),
            pltpu.VMEM((2, rb, dw), U32), pltpu.VMEM((2, rb, dh), U32),
            pltpu.VMEM((SUBLANES, rb), I32), pltpu.SMEM((rb,), I32),
            pltpu.SemaphoreType.DMA((2,)), pltpu.SemaphoreType.DMA((2,)), pltpu.SemaphoreType.DMA(()),
            pltpu.SMEM((3,), I32),
        ],
    )
    return pl.pallas_call(
        _expert_kernel,
        grid_spec=grid_spec,
        out_shape=jax.ShapeDtypeStruct((TOP_K * n_tok + 2 * rb, dh), U32),
        compiler_params=_cparams(("arbitrary",)),
        name="moe_experts",
    )(starts, counts, xs, w_gate, w_up, w_down)


def _combine_kernel(*refs):
    y_refs = refs[:TOP_K]
    h_ref, wg_ref, wu_ref, wd_ref, g_ref, b_ref, o_ref, wgb_ref, wub_ref, wdb_ref = refs[TOP_K:]

    @pl.when(pl.program_id(0) == 0)
    def _():
        wgb_ref[...] = wg_ref[...].astype(BF16)
        wub_ref[...] = wu_ref[...].astype(BF16)
        wdb_ref[...] = wd_ref[...].astype(BF16)

    h = h_ref[...]
    hb = h.astype(BF16)
    gate = jnp.dot(hb, wgb_ref[...], preferred_element_type=F32)
    up = jnp.dot(hb, wub_ref[...], preferred_element_type=F32)
    hdn = (_silu(gate) * up).astype(BF16)
    shared = jnp.dot(hdn, wdb_ref[...], preferred_element_type=F32)

    r_lo, r_hi = _unpack_rows(y_refs[0][...])
    for k in range(1, TOP_K):
        lo, hi = _unpack_rows(y_refs[k][...])
        r_lo = r_lo + lo
        r_hi = r_hi + hi
    ff = jnp.concatenate([r_lo, r_hi], axis=1) + shared
    o_ref[...] = _layer_norm(ALPHA * h + ff, g_ref[...], b_ref[...])


def _combine(y, h, ws_gate, ws_up, ws_down, g, b):
    s, d = h.shape
    f = ws_gate.shape[1]
    tt = COMBINE_TT
    row = pl.BlockSpec((tt, d), lambda i: (i, 0))
    vec = pl.BlockSpec((1, d), lambda i: (0, 0))
    planes = [pl.BlockSpec((tt, d // 2), functools.partial(lambda k, i: (k * (s // tt) + i, 0), k))
              for k in range(TOP_K)]
    return pl.pallas_call(
        _combine_kernel,
        grid=(s // tt,),
        in_specs=planes + [
            row,
            pl.BlockSpec((d, f), lambda i: (0, 0)),
            pl.BlockSpec((d, f), lambda i: (0, 0)),
            pl.BlockSpec((f, d), lambda i: (0, 0)),
            vec, vec,
        ],
        out_specs=row,
        out_shape=jax.ShapeDtypeStruct((s, d), F32),
        scratch_shapes=[pltpu.VMEM((d, f), BF16), pltpu.VMEM((d, f), BF16), pltpu.VMEM((f, d), BF16)],
        compiler_params=_cparams(("arbitrary",)),
        name="moe_combine_ln2",
    )(*([y] * TOP_K), h, ws_gate, ws_up, ws_down, g, b)


def _rope_tables(positions):
    half = DA_QK_DIM // 2
    inv = ROPE_THETA ** (-jnp.arange(0, DA_QK_DIM, 2, dtype=F32) / DA_QK_DIM)
    ang = positions.astype(F32)[:, None] * inv
    cos = jnp.tile(jnp.cos(ang), (1, LANES // half))
    sin = jnp.sin(ang)
    sin = jnp.tile(jnp.concatenate([-sin, sin], axis=-1), (1, LANES // DA_QK_DIM))
    return cos, sin


def _moe(h, w_router, e_bias, w_gate, w_up, w_down, ws_gate, ws_up, ws_down, ln_g, ln_b):
    s, d = h.shape
    e = w_router.shape[1]
    idx, wt, rank, counts = _router(h, w_router, e_bias.reshape(1, e))
    counts = counts[0].astype(jnp.int32)
    acounts = (counts + ROW_ALIGN - 1) // ROW_ALIGN * ROW_ALIGN
    ends = jnp.cumsum(acounts)
    starts = (ends - acounts).astype(jnp.int32)
    dest_t = _dest(idx, rank, starts.astype(F32).reshape(1, e))
    n_rows = s * TOP_K + e * ROW_ALIGN + MOE_RB

    xs = _dispatch(dest_t, h, idx, wt, n_rows)
    y = _experts(starts, counts, xs, w_gate, w_up, w_down, s)
    return _combine(y, h, ws_gate, ws_up, ws_down, ln_g.reshape(1, d), ln_b.reshape(1, d))


def kernel(x, positions, ln_in_g, ln_in_b, w_in, w_out, lam_q1, lam_k1, lam_q2, lam_k2, da_norm_g,
           hg_lb_logits, hg_norm_g, ln1_g, ln1_b, w_router, e_bias, w_gate, w_up, w_down,
           ws_gate, ws_up, ws_down, ln2_g, ln2_b):
    bsz, s, d = x.shape
    assert bsz == 1 and w_in.shape[0] == DEPTH
    x2 = x.reshape(s, d)
    cos, sin = _rope_tables(positions.reshape(s))
    lower_bounds = jnp.cumsum(jax.nn.softmax(hg_lb_logits.astype(F32), axis=0), axis=0)
    lam = (jnp.exp(jnp.sum(lam_q1[0] * lam_k1[0])) - jnp.exp(jnp.sum(lam_q2[0] * lam_k2[0]))
           + LAMBDA_INIT).reshape(1).astype(F32)
    gi, bi = ln_in_g.reshape(1, d), ln_in_b.reshape(1, d)

    da_w = DA_HEADS * LANES
    hb = _ln_in(x2, gi, bi)
    w_in_b = w_in[0].astype(BF16)
    w_out_b = w_out[0].astype(BF16)
    qm = _proj("q", hb, w_in_b, 0, da_w, cos, sin)
    kr = _proj("k", hb, w_in_b, da_w, da_w, cos, sin)
    vb = _proj("v", hb, w_in_b, 2 * da_w, da_w)
    proj_h = _proj("plain", hb, w_in_b, 3 * da_w, w_in.shape[2] - 3 * da_w)
    y_a = _attention(lam, qm, kr, vb, da_norm_g[0].reshape(1, LANES))
    y_b = _hgrn(proj_h, lower_bounds[0].reshape(1, -1), hg_norm_g[0].reshape(1, LANES), 0)
    h1 = _outproj(y_a, y_b, w_out_b, x2, gi, bi, ln1_g[0].reshape(1, d), ln1_b[0].reshape(1, d))
    h2 = _moe(h1, w_router[0], e_bias[0], w_gate[0], w_up[0], w_down[0],
              ws_gate[0], ws_up[0], ws_down[0], ln2_g[0], ln2_b[0])
    return h2.reshape(bsz, s, d)
```

```python
import functools
import math

import jax
import jax.numpy as jnp
import numpy as np
from jax import lax
from jax.experimental import pallas as pl
from jax.experimental.pallas import tpu as pltpu

F32 = jnp.float32
BF16 = jnp.bfloat16
U32 = jnp.uint32
I32 = jnp.int32
HI_MASK = np.uint32(0xFFFF0000)

CHUNK = 64
DA_HEADS = 8
DA_QK_DIM = 64
DA_V_DIM = 2 * DA_QK_DIM
HG_HEADS = 8
HG_DIM = 128
ROPE_THETA = 10000.0
N_EXPERTS = 256
TOP_K = 8
N_GROUPS = 8
TOPK_GROUPS = 4
GROUP_SIZE = N_EXPERTS // N_GROUPS
ROUTED_SCALE = 2.5
DEPTH = 1
ALPHA = (2.0 * DEPTH) ** 0.25
LN_EPS = 1e-5
RMS_EPS = 1e-5
LAMBDA_INIT = 0.8 - 0.6 * math.exp(-0.3 * 0)

LANES = 128
VMEM_LIMIT = 56 * 1024 * 1024
NEG = -1e30
DMA_PRIORITIES = 2

LN_TM = 512
INPROJ_TM, INPROJ_TN = 1024, 512
ATTN_T = 1024
HG_TR = 512
HG_SUB = 16
HG_GRP = 128
OUT_TM, OUT_TK = 512, 1024
ROUTER_TM = 512
MOE_RB = 384
MOE_CH = 64
SUBLANES = 8
ROW_ALIGN = SUBLANES
TAG_EXPERT0 = 8
DISPATCH_TT = 256
COMBINE_TT = 128


def _cparams(sem):
    return pltpu.CompilerParams(dimension_semantics=sem, vmem_limit_bytes=VMEM_LIMIT)


def _layer_norm(xf, g, b):
    mu = jnp.mean(xf, axis=-1, keepdims=True)
    xc = xf - mu
    var = jnp.mean(xc * xc, axis=-1, keepdims=True)
    return xc * lax.rsqrt(var + LN_EPS) * g + b


def _silu(x):
    return x * jax.nn.sigmoid(x)


def _ln_in_kernel(x_ref, g_ref, b_ref, o_ref):
    o_ref[...] = _layer_norm(x_ref[...], g_ref[...], b_ref[...]).astype(o_ref.dtype)


def _ln_in(x, g, b):
    s, d = x.shape
    tm = LN_TM
    vec = pl.BlockSpec((1, d), lambda i: (0, 0))
    return pl.pallas_call(
        _ln_in_kernel,
        grid=(s // tm,),
        in_specs=[pl.BlockSpec((tm, d), lambda i: (i, 0)), vec, vec],
        out_specs=pl.BlockSpec((tm, d), lambda i: (i, 0)),
        out_shape=jax.ShapeDtypeStruct((s, d), BF16),
        compiler_params=_cparams(("arbitrary",)),
        name="ln_in",
    )(x, g, b)


def _rope(x, cos, sin, first_half):
    rot = jnp.where(first_half, pltpu.roll(x, LANES - DA_QK_DIM // 2, 1), pltpu.roll(x, DA_QK_DIM // 2, 1))
    return x * cos + rot * sin


def _proj_kernel(role, h_ref, w_ref, *rest):
    acc = jnp.dot(h_ref[...], w_ref[...], preferred_element_type=F32)
    if role == "plain":
        (o_ref,) = rest
        o_ref[...] = acc
        return
    heads = acc.shape[1] // LANES
    if role == "v":
        (o_ref,) = rest
        for hh in range(heads):
            o_ref[hh] = acc[:, hh * LANES:(hh + 1) * LANES].astype(BF16)
        return
    cos_ref, sin_ref, o_ref = rest
    cos = cos_ref[...]
    sin = sin_ref[...]
    lane = lax.broadcasted_iota(jnp.int32, cos.shape, 1)
    first_half = (lane % DA_QK_DIM) < (DA_QK_DIM // 2)
    map0 = lane < DA_QK_DIM
    for hh in range(heads):
        r = _rope(acc[:, hh * LANES:(hh + 1) * LANES], cos, sin, first_half)
        if role == "k":
            o_ref[hh] = r.astype(BF16)
        else:
            r = r * (DA_QK_DIM ** -0.5)
            o_ref[hh, 0] = jnp.where(map0, r, 0.0).astype(BF16)
            o_ref[hh, 1] = jnp.where(map0, 0.0, r).astype(BF16)


def _proj(role, hb, w, col0, ncols, cos=None, sin=None):
    s, d = hb.shape
    tm, tn = INPROJ_TM, INPROJ_TN
    jb = col0 // tn
    hpt = tn // LANES
    nh = ncols // LANES
    in_specs = [pl.BlockSpec((tm, d), lambda i, j: (i, 0)), pl.BlockSpec((d, tn), lambda i, j: (0, jb + j))]
    args = [hb, w]
    if role in ("q", "k"):
        tab = pl.BlockSpec((tm, LANES), lambda i, j: (i, 0))
        in_specs += [tab, tab]
        args += [cos, sin]
    if role == "plain":
        out_spec = pl.BlockSpec((tm, tn), lambda i, j: (i, j))
        out_shape = jax.ShapeDtypeStruct((s, ncols), F32)
    elif role == "q":
        out_spec = pl.BlockSpec((hpt, 2, tm, LANES), lambda i, j: (j, 0, i, 0))
        out_shape = jax.ShapeDtypeStruct((nh, 2, s, LANES), BF16)
    else:
        out_spec = pl.BlockSpec((hpt, tm, LANES), lambda i, j: (j, i, 0))
        out_shape = jax.ShapeDtypeStruct((nh, s, LANES), BF16)
    return pl.pallas_call(
        functools.partial(_proj_kernel, role),
        grid=(s // tm, ncols // tn),
        in_specs=in_specs,
        out_specs=out_spec,
        out_shape=out_shape,
        compiler_params=_cparams(("arbitrary", "arbitrary")),
        name="inproj_" + role,
    )(*args)


def _attn_kernel(lam_ref, q_ref, k_ref, v_ref, g_ref, o_ref, acc_ref, m_ref, l_ref):
    t = ATTN_T
    i = pl.program_id(1)
    hf = t // 2
    qa = q_ref[0]
    q = jnp.concatenate([qa[0, :hf], qa[1, :hf], qa[0, hf:], qa[1, hf:]], axis=0)
    m_ref[...] = jnp.full(m_ref.shape, NEG, F32)
    l_ref[...] = jnp.zeros(l_ref.shape, F32)
    acc_ref[...] = jnp.zeros(acc_ref.shape, F32)

    def step(start, width, rows, masked):
        k = k_ref[0, pl.ds(start, width), :]
        v = v_ref[0, pl.ds(start, width), :]
        s = lax.dot_general(q[rows], k, (((1,), (1,)), ((), ())), preferred_element_type=F32)
        if masked:
            row = lax.broadcasted_iota(jnp.int32, s.shape, 0) % hf
            col = lax.broadcasted_iota(jnp.int32, s.shape, 1)
            s = jnp.where((col // CHUNK) <= (row // CHUNK), s, NEG)
        cols = [s[:, j * LANES:(j + 1) * LANES] for j in range(width // LANES)]
        mx = cols[0]
        for cj in cols[1:]:
            mx = jnp.maximum(mx, cj)
        m_prev = m_ref[rows, :]
        m_new = jnp.maximum(m_prev, jnp.max(mx, axis=-1, keepdims=True))
        alpha = jnp.exp(m_prev - m_new)
        ps = [jnp.exp(cj - m_new) for cj in cols]
        lsum = ps[0]
        for pj in ps[1:]:
            lsum = lsum + pj
        l_ref[rows, :] = alpha * l_ref[rows, :] + lsum
        m_ref[rows, :] = m_new
        p = jnp.concatenate([pj.astype(BF16) for pj in ps], axis=-1)
        acc_ref[rows, :] = alpha * acc_ref[rows, :] + jnp.dot(p, v, preferred_element_type=F32)

    def body(c, carry):
        step(pl.multiple_of(c * t, t), t, slice(0, 2 * t), False)
        return carry

    lax.fori_loop(0, i, body, 0)
    d0 = pl.multiple_of(i * t, t)
    step(d0, hf, slice(0, t), True)
    step(d0, hf, slice(t, 2 * t), False)
    step(pl.multiple_of(d0 + hf, hf), hf, slice(t, 2 * t), True)

    lam = lam_ref[0]
    acc = acc_ref[...]
    l = jnp.sum(l_ref[...], axis=-1, keepdims=True)
    o1 = jnp.concatenate([acc[:hf] / l[:hf], acc[t:t + hf] / l[t:t + hf]], axis=0)
    o2 = jnp.concatenate([acc[hf:t] / l[hf:t], acc[t + hf:] / l[t + hf:]], axis=0)
    o = o1 - lam * o2
    o = o * lax.rsqrt(jnp.mean(o * o, axis=-1, keepdims=True) + RMS_EPS) * g_ref[...] * (1.0 - LAMBDA_INIT)
    o_ref[...] = o.astype(o_ref.dtype)


def _attention(lam, qm, kr, vb, norm_g):
    h, _, s, _ = qm.shape
    t = ATTN_T
    return pl.pallas_call(
        _attn_kernel,
        grid=(h, s // t),
        in_specs=[
            pl.BlockSpec(memory_space=pltpu.SMEM),
            pl.BlockSpec((1, 2, t, LANES), lambda hh, i: (hh, 0, i, 0)),
            pl.BlockSpec((1, s, LANES), lambda hh, i: (hh, 0, 0)),
            pl.BlockSpec((1, s, LANES), lambda hh, i: (hh, 0, 0)),
            pl.BlockSpec((1, LANES), lambda hh, i: (0, 0)),
        ],
        out_specs=pl.BlockSpec((t, LANES), lambda hh, i: (i, hh)),
        out_shape=jax.ShapeDtypeStruct((s, h * LANES), BF16),
        scratch_shapes=[
            pltpu.VMEM((2 * t, LANES), F32),
            pltpu.VMEM((2 * t, LANES), F32),
            pltpu.VMEM((2 * t, LANES), F32),
        ],
        compiler_params=_cparams(("arbitrary", "arbitrary")),
        name="diff_attn",
    )(lam, qm, kr, vb, norm_g)


def _hgrn_kernel(q_ref, f_ref, i_ref, g_ref, lb_ref, ng_ref, o_ref,
                 st_ref, qs_ref, kk_ref, b_ref, kh_ref, od_ref, iv_ref):
    tr, sub, grp = HG_TR, HG_SUB, HG_GRP

    @pl.when(pl.program_id(1) == 0)
    def _():
        st_ref[...] = jnp.zeros(st_ref.shape, F32)

    lb = lb_ref[...]
    f = lb + (1.0 - lb) * jax.nn.sigmoid(f_ref[...])
    log_f = jnp.log(f)
    kk = 1.0 - f
    qs_ref[...] = _silu(q_ref[...])
    zpad = jnp.zeros((sub, LANES), F32)
    kk_ref[0:sub, :] = zpad
    b_ref[0:sub, :] = zpad
    iv_ref[0:sub, :] = zpad
    kk_ref[sub:, :] = kk
    iv_ref[sub:, :] = i_ref[...]

    r = lax.broadcasted_iota(jnp.int32, (grp, grp), 0)
    c = lax.broadcasted_iota(jnp.int32, (grp, grp), 1)
    same = (r // sub) == (c // sub)
    tri = jnp.where(same & (c <= r), 1.0, 0.0).astype(F32)
    blk = jnp.where(same, 1.0, 0.0).astype(F32)
    grp_iota = lax.broadcasted_iota(jnp.int32, (grp, LANES), 0)
    pos_in_sub = grp_iota % sub

    for gi in range(tr // grp):
        g0 = gi * grp
        rows = slice(g0, g0 + grp)
        prow = slice(sub + g0, sub + g0 + grp)
        lf = log_f[rows]
        bg = jnp.dot(tri, lf, preferred_element_type=F32, precision=lax.Precision.HIGHEST)
        tot = jnp.dot(blk, lf, preferred_element_type=F32, precision=lax.Precision.HIGHEST)
        b_ref[prow, :] = bg
        kh_ref[rows, :] = kk[rows] * jnp.exp(tot - bg)
        qg = qs_ref[rows, :]
        acc = jnp.sum(qg * kk[rows], axis=-1, keepdims=True) * i_ref[rows, :]
        for lag in range(1, sub):
            srow = slice(sub + g0 - lag, sub + g0 - lag + grp)
            e = jnp.exp(jnp.where(pos_in_sub >= lag, bg - b_ref[srow, :], NEG))
            w = jnp.sum(qg * kk_ref[srow, :] * e, axis=-1, keepdims=True)
            acc = acc + w * iv_ref[srow, :]
        od_ref[rows, :] = acc

    st = st_ref[...]
    for gi in range(tr // grp):
        g0 = gi * grp
        iv_t = i_ref[g0:g0 + grp, :].T.astype(BF16)
        kh_g = kh_ref[g0:g0 + grp, :]
        upds = []
        for j in range(grp // sub):
            in_sub = (grp_iota >= j * sub) & (grp_iota < (j + 1) * sub)
            upds.append(jnp.dot(iv_t, jnp.where(in_sub, kh_g, 0.0).astype(BF16), preferred_element_type=F32))
        for j in range(grp // sub):
            base = g0 + j * sub
            bj = b_ref[sub + base:sub + base + sub, :]
            qj = qs_ref[base:base + sub, :]
            o_inter = lax.dot_general((qj * jnp.exp(bj)).astype(BF16), st.astype(BF16),
                                      (((1,), (1,)), ((), ())), preferred_element_type=F32)
            od_ref[base:base + sub, :] = od_ref[base:base + sub, :] + o_inter
            st = st * jnp.exp(bj[sub - 1:sub]) + upds[j]
    st_ref[...] = st

    o = od_ref[...]
    o = o * lax.rsqrt(jnp.mean(o * o, axis=-1, keepdims=True) + RMS_EPS) * ng_ref[...]
    o_ref[...] = (o * _silu(g_ref[...])).astype(o_ref.dtype)


def _hgrn(proj, lower_bound, norm_g, col0):
    s = proj.shape[0]
    tr, h = HG_TR, HG_HEADS
    blk = lambda off: pl.BlockSpec((tr, LANES), lambda hh, i: (i, col0 + off + hh))
    pad = pltpu.VMEM((tr + HG_SUB, LANES), F32)
    full = pltpu.VMEM((tr, LANES), F32)
    return pl.pallas_call(
        _hgrn_kernel,
        grid=(h, s // tr),
        in_specs=[blk(0), blk(h), blk(2 * h), blk(3 * h),
                  pl.BlockSpec((1, LANES), lambda hh, i: (0, hh)),
                  pl.BlockSpec((1, LANES), lambda hh, i: (0, 0))],
        out_specs=pl.BlockSpec((tr, LANES), lambda hh, i: (i, hh)),
        out_shape=jax.ShapeDtypeStruct((s, h * LANES), BF16),
        scratch_shapes=[pltpu.VMEM((HG_DIM, HG_DIM), F32), full, pad, pad, full, full, pad],
        compiler_params=_cparams(("arbitrary", "arbitrary")),
        name="hgrn2",
    )(proj, proj, proj, proj, lower_bound, norm_g)


def _outproj_kernel(ya_ref, yb_ref, w_ref, x_ref, gi_ref, bi_ref, g1_ref, b1_ref, h_ref, acc_ref):
    k = pl.program_id(1)
    nk = pl.num_programs(1)

    @pl.when(k == 0)
    def _():
        acc_ref[...] = jnp.zeros(acc_ref.shape, F32)

    w = w_ref[...]

    @pl.when(k < nk // 2)
    def _():
        acc_ref[...] += jnp.dot(ya_ref[...], w, preferred_element_type=F32)

    @pl.when(k >= nk // 2)
    def _():
        acc_ref[...] += jnp.dot(yb_ref[...], w, preferred_element_type=F32)

    @pl.when(k == nk - 1)
    def _():
        h0 = _layer_norm(x_ref[...], gi_ref[...], bi_ref[...])
        h_ref[...] = _layer_norm(ALPHA * h0 + acc_ref[...], g1_ref[...], b1_ref[...])


def _outproj(ya, yb, w, x, gi, bi, g1, b1):
    s, d = x.shape
    tm, tk = OUT_TM, OUT_TK
    nka = ya.shape[1] // tk
    nk = w.shape[0] // tk
    assert nk == 2 * nka and yb.shape[1] == ya.shape[1]
    vec = pl.BlockSpec((1, d), lambda i, k: (0, 0))
    return pl.pallas_call(
        _outproj_kernel,
        grid=(s // tm, nk),
        in_specs=[
            pl.BlockSpec((tm, tk), lambda i, k: (i, jnp.minimum(k, nka - 1))),
            pl.BlockSpec((tm, tk), lambda i, k: (i, jnp.maximum(k - nka, 0))),
            pl.BlockSpec((tk, d), lambda i, k: (k, 0)),
            pl.BlockSpec((tm, d), lambda i, k: (i, 0)),
            vec, vec, vec, vec,
        ],
        out_specs=pl.BlockSpec((tm, d), lambda i, k: (i, 0)),
        out_shape=jax.ShapeDtypeStruct((s, d), F32),
        scratch_shapes=[pltpu.VMEM((tm, d), F32)],
        compiler_params=_cparams(("arbitrary", "arbitrary")),
        name="outproj_ln1",
    )(ya, yb, w, x, gi, bi, g1, b1)


def _router_kernel(h_ref, w_ref, eb_ref, idx_ref, wt_ref, rank_ref, cnt_ref, carry_ref):
    tm = ROUTER_TM
    i = pl.program_id(0)

    @pl.when(i == 0)
    def _():
        carry_ref[...] = jnp.zeros(carry_ref.shape, F32)

    logits = jnp.dot(h_ref[...], w_ref[...], preferred_element_type=F32, precision=lax.Precision.HIGHEST)
    scores = jax.nn.sigmoid(logits)
    choice = scores + eb_ref[...]
    lane = lax.broadcasted_iota(jnp.int32, choice.shape, 1)
    grp_of_lane = lane // GROUP_SIZE
    big = jnp.int32(N_EXPERTS)
    ninf = -jnp.inf

    def first_argmax(vals):
        m = jnp.max(vals, axis=-1, keepdims=True)
        idx = jnp.min(jnp.where(vals == m, lane, big), axis=-1, keepdims=True)
        return m, idx

    gs = []
    for g in range(N_GROUPS):
        vals = jnp.where(grp_of_lane == g, choice, ninf)
        m1, i1 = first_argmax(vals)
        m2 = jnp.max(jnp.where(lane == i1, ninf, vals), axis=-1, keepdims=True)
        gs.append(m1 + m2)
    allowed = jnp.zeros(choice.shape, jnp.int32)
    for g in range(N_GROUPS):
        beats = jnp.zeros(gs[g].shape, jnp.int32)
        for g2 in range(N_GROUPS):
            if g2 == g:
                continue
            better = (gs[g2] >= gs[g]) if g2 < g else (gs[g2] > gs[g])
            beats = beats + jnp.where(better, 1, 0)
        allowed = jnp.where(grp_of_lane == g, jnp.where(beats < TOPK_GROUPS, 1, 0), allowed)
    masked = jnp.where(allowed > 0, choice, ninf)

    sel = jnp.zeros(choice.shape, F32)
    idxs, wts = [], []
    for _ in range(TOP_K):
        _, ik = first_argmax(masked)
        hit = lane == ik
        idxs.append(ik)
        wts.append(jnp.sum(jnp.where(hit, scores, 0.0), axis=-1, keepdims=True))
        sel = jnp.where(hit, 1.0, sel)
        masked = jnp.where(hit, ninf, masked)
    wsum = wts[0]
    for k in range(1, TOP_K):
        wsum = wsum + wts[k]

    r = lax.broadcasted_iota(jnp.int32, (tm, tm), 0)
    c = lax.broadcasted_iota(jnp.int32, (tm, tm), 1)
    strict_lower = jnp.where(c < r, 1.0, 0.0).astype(BF16)
    rank = jnp.dot(strict_lower, sel.astype(BF16), preferred_element_type=F32) + carry_ref[...]
    carry_ref[...] = carry_ref[...] + jnp.sum(sel, axis=0, keepdims=True)
    cnt_ref[...] = carry_ref[...]

    out_lane = lax.broadcasted_iota(jnp.int32, (tm, LANES), 1)
    idx_out = jnp.zeros((tm, LANES), jnp.int32)
    wt_out = jnp.zeros((tm, LANES), F32)
    rank_out = jnp.zeros((tm, LANES), jnp.int32)
    for k in range(TOP_K):
        rk = jnp.sum(jnp.where(lane == idxs[k], rank, 0.0), axis=-1, keepdims=True)
        idx_out = jnp.where(out_lane == k, idxs[k], idx_out)
        wt_out = jnp.where(out_lane == k, wts[k] / wsum * ROUTED_SCALE, wt_out)
        rank_out = jnp.where(out_lane == k, rk.astype(jnp.int32), rank_out)
    idx_ref[...] = idx_out
    wt_ref[...] = wt_out
    rank_ref[...] = rank_out


def _router(h, w_router, e_bias):
    s, d = h.shape
    e = w_router.shape[1]
    tm = ROUTER_TM
    row = pl.BlockSpec((tm, LANES), lambda i: (i, 0))
    return pl.pallas_call(
        _router_kernel,
        grid=(s // tm,),
        in_specs=[
            pl.BlockSpec((tm, d), lambda i: (i, 0)),
            pl.BlockSpec((d, e), lambda i: (0, 0)),
            pl.BlockSpec((1, e), lambda i: (0, 0)),
        ],
        out_specs=[row, row, row, pl.BlockSpec((1, e), lambda i: (0, 0))],
        out_shape=[
            jax.ShapeDtypeStruct((s, LANES), jnp.int32),
            jax.ShapeDtypeStruct((s, LANES), F32),
            jax.ShapeDtypeStruct((s, LANES), jnp.int32),
            jax.ShapeDtypeStruct((1, e), F32),
        ],
        scratch_shapes=[pltpu.VMEM((1, e), F32)],
        compiler_params=_cparams(("arbitrary",)),
        name="router_topk",
    )(h, w_router, e_bias)


def _dest_kernel(idx_ref, rank_ref, ps_ref, dest_ref):
    idx = idx_ref[...]
    rank = rank_ref[...]
    ps = ps_ref[...]
    lane = lax.broadcasted_iota(jnp.int32, (idx.shape[0], ps.shape[1]), 1)
    out_lane = lax.broadcasted_iota(jnp.int32, idx.shape, 1)
    dest = jnp.zeros(idx.shape, jnp.int32)
    for k in range(TOP_K):
        start_k = jnp.sum(jnp.where(lane == idx[:, k:k + 1], ps, 0.0), axis=-1, keepdims=True)
        dest = jnp.where(out_lane == k, start_k.astype(jnp.int32) + rank, dest)
    dest_ref[...] = dest.T[:TOP_K, :]


def _dest(idx, rank, pstarts):
    s = idx.shape[0]
    e = pstarts.shape[1]
    tm = ROUTER_TM
    row = pl.BlockSpec((tm, LANES), lambda i: (i, 0))
    return pl.pallas_call(
        _dest_kernel,
        grid=(s // tm,),
        in_specs=[row, row, pl.BlockSpec((1, e), lambda i: (0, 0))],
        out_specs=pl.BlockSpec((TOP_K, tm), lambda i: (0, i)),
        out_shape=jax.ShapeDtypeStruct((TOP_K, s), jnp.int32),
        compiler_params=_cparams(("arbitrary",)),
        name="moe_dest",
    )(idx, rank, pstarts)


def _pack_rows(x):
    half = x.shape[1] // 2
    lo = lax.bitcast_convert_type(x[:, :half].astype(BF16).astype(F32), U32) >> 16
    hi = lax.bitcast_convert_type(x[:, half:].astype(BF16).astype(F32), U32) & HI_MASK
    return lo | hi


def _unpack_rows(w):
    lo = lax.bitcast_convert_type(w << 16, F32)
    hi = lax.bitcast_convert_type(w & HI_MASK, F32)
    return lo, hi


def _dispatch_kernel(dest_ref, h_ref, idx_ref, wt_ref, xs_ref, hp_ref, sem):
    tt = DISPATCH_TT
    dh = h_ref.shape[1] // 2
    hp_ref[:, :dh] = _pack_rows(h_ref[...])
    lane = lax.broadcasted_iota(I32, (tt, LANES), 1)
    tok = lax.broadcasted_iota(I32, (tt, LANES), 0) + pl.program_id(0) * tt
    experts = pltpu.roll(idx_ref[...], TAG_EXPERT0, 1)
    weights = pltpu.roll(lax.bitcast_convert_type(wt_ref[...], I32), TAG_EXPERT0 + TOP_K, 1)
    tag = jnp.where(lane == 0, tok,
                    jnp.where((lane >= TAG_EXPERT0) & (lane < TAG_EXPERT0 + TOP_K), experts,
                              jnp.where((lane >= TAG_EXPERT0 + TOP_K) & (lane < TAG_EXPERT0 + 2 * TOP_K), weights, 0)))
    hp_ref[:, dh:] = lax.bitcast_convert_type(tag, U32)

    def row_copy(t, k):
        return pltpu.make_async_copy(hp_ref.at[pl.ds(t, 1)], xs_ref.at[pl.ds(dest_ref[k, t], 1)], sem)

    def drain(t, carry):
        for k in range(TOP_K):
            row_copy(t, k).wait()
        return carry

    for t in range(tt):
        for k in range(TOP_K):
            row_copy(t, k).start(priority=(t * TOP_K + k) % DMA_PRIORITIES)
    lax.fori_loop(0, tt, drain, 0)


def _dispatch(dest_t, h, idx, wt, n_rows):
    s, d = h.shape
    tt = DISPATCH_TT
    row = pl.BlockSpec((tt, LANES), lambda i: (i, 0))
    return pl.pallas_call(
        _dispatch_kernel,
        grid=(s // tt,),
        in_specs=[
            pl.BlockSpec((TOP_K, tt), lambda i: (0, i), memory_space=pltpu.SMEM),
            pl.BlockSpec((tt, d), lambda i: (i, 0)),
            row, row,
        ],
        out_specs=pl.BlockSpec(memory_space=pl.ANY),
        out_shape=jax.ShapeDtypeStruct((n_rows, d // 2 + LANES), U32),
        scratch_shapes=[pltpu.VMEM((tt, d // 2 + LANES), U32), pltpu.SemaphoreType.DMA(())],
        compiler_params=_cparams(("arbitrary",)),
        name="moe_dispatch",
    )(dest_t, h, idx, wt)


def _expert_kernel(start_ref, cnt_ref, xs_ref, wg_ref, wu_ref, wd_ref, y_ref,
                   wgb_ref, wub_ref, wdb_ref, wd_buf_ref, xin_ref, yout_ref, slot_v_ref, slot_s_ref,
                   in_sem, out_sem, slot_sem, wd_sem, st_ref):
    e = pl.program_id(0)
    ne = pl.num_programs(0)
    rb, ch = MOE_RB, MOE_CH
    dh = yout_ref.shape[2]
    n_tok = (y_ref.shape[0] - 2 * rb) // TOP_K
    wd_slot = e % 2

    def wd_copy(expert, slot):
        return pltpu.make_async_copy(wd_ref.at[expert], wd_buf_ref.at[slot], wd_sem.at[slot])

    @pl.when(e == 0)
    def _():
        wd_copy(0, 0).start(priority=DMA_PRIORITIES - 1)

    @pl.when(e + 1 < ne)
    def _():
        wd_copy(e + 1, 1 - wd_slot).start(priority=DMA_PRIORITIES - 1)

    def n_chunks(cnt_e, j):
        return jnp.clip((cnt_e - j * rb + ch - 1) // ch, 0, rb // ch)

    def in_chunk(row0, slot, c):
        return pltpu.make_async_copy(xs_ref.at[pl.ds(pl.multiple_of(row0 + c * ch, ROW_ALIGN), ch)],
                                     xin_ref.at[slot, pl.ds(pl.multiple_of(c * ch, ch), ch)], in_sem.at[slot])

    def out_row(slot, r, dst):
        return pltpu.make_async_copy(yout_ref.at[slot, pl.ds(r, 1)], y_ref.at[pl.ds(dst, 1)], out_sem.at[slot])

    def scatter_start(n_groups, slot):
        for g in range(rb // SUBLANES):
            @pl.when(g < n_groups)
            def _():
                for u in range(SUBLANES):
                    r = g * SUBLANES + u
                    out_row(slot, r, slot_s_ref[r]).start(priority=DMA_PRIORITIES - 1)

    def scatter_wait(n_groups, slot):
        def body(g, carry):
            for u in range(SUBLANES):
                out_row(slot, 0, 0).wait()
            return carry
        lax.fori_loop(0, n_groups, body, 0)

    def for_chunks(n, fn):
        def body(c, carry):
            fn(c)
            return carry
        lax.fori_loop(0, n, body, 0)

    @pl.when(e == 0)
    def _():
        st_ref[0] = 0
        st_ref[1] = 0
        st_ref[2] = 0
        for_chunks(n_chunks(cnt_ref[0], 0), lambda c: in_chunk(start_ref[0], 0, c).start())

    start = start_ref[e]
    cnt = cnt_ref[e]
    n_sub = jnp.maximum(1, (cnt + rb - 1) // rb)
    e_next = jnp.minimum(e + 1, ne - 1)
    next_start = start_ref[e_next]
    next_first = jnp.where(e == ne - 1, 0, n_chunks(cnt_ref[e_next], 0))
    lane = lax.broadcasted_iota(I32, (rb, LANES), 1)

    def sub_block(j, carry, first=False):
        slot = st_ref[0]
        row0 = start + j * rb
        n_cur = n_chunks(cnt, j)
        n_live = jnp.clip(cnt - j * rb, 0, rb)
        last_j = j == n_sub - 1
        nxt_row = jnp.where(last_j, next_start, row0 + rb)
        n_nxt = jnp.where(last_j, next_first, n_chunks(cnt, j + 1))
        for_chunks(n_nxt, lambda c: in_chunk(nxt_row, 1 - slot, c).start())
        for_chunks(n_cur, lambda c: in_chunk(row0, slot, c).wait())

        tag = lax.bitcast_convert_type(xin_ref[slot, :, dh:], I32)
        mine = (tag == e) & (lane >= TAG_EXPERT0) & (lane < TAG_EXPERT0 + TOP_K)
        k_sel = jnp.sum(jnp.where(mine, lane - TAG_EXPERT0, 0).astype(F32), axis=-1, keepdims=True)
        w_sel = jnp.sum(jnp.where(pltpu.roll(jnp.where(mine, 1, 0), TOP_K, 1) > 0,
                                  lax.bitcast_convert_type(tag, F32), 0.0), axis=-1, keepdims=True)
        tok = jnp.sum(jnp.where(lane == 0, tag, 0).astype(F32), axis=-1, keepdims=True)
        row_i = lax.broadcasted_iota(I32, (rb, 1), 0)
        dst = jnp.where(row_i < n_live, (k_sel * n_tok + tok).astype(I32), TOP_K * n_tok + slot * rb + row_i)
        for g in range(rb // LANES):
            blk = jnp.broadcast_to(dst[g * LANES:(g + 1) * LANES], (LANES, LANES))
            slot_v_ref[:, g * LANES:(g + 1) * LANES] = blk.T[0:SUBLANES, :]
        slot_copy = pltpu.make_async_copy(slot_v_ref.at[0], slot_s_ref, slot_sem)
        slot_copy.start()

        xw = xin_ref[slot, :, :dh]
        live = (lax.broadcasted_iota(I32, xw.shape, 0) + j * rb) < cnt
        lo, hi = _unpack_rows(jnp.where(live, xw, jnp.zeros_like(xw)))
        xb = jnp.concatenate([lo.astype(BF16), hi.astype(BF16)], axis=1)
        if first:
            wgb_ref[...] = wg_ref[0].astype(BF16)
        gate = jnp.dot(xb, wgb_ref[...], preferred_element_type=F32)
        if first:
            wub_ref[...] = wu_ref[0].astype(BF16)
        up = jnp.dot(xb, wub_ref[...], preferred_element_type=F32)
        if first:
            wd_copy(e, wd_slot).wait()
            wdb_ref[...] = wd_buf_ref[wd_slot].astype(BF16)
        hdn = (_silu(gate) * up).astype(BF16)
        y = jnp.dot(hdn, wdb_ref[...], preferred_element_type=F32) * w_sel

        scatter_wait(st_ref[1 + slot], slot)
        yout_ref[slot] = _pack_rows(y)
        slot_copy.wait()
        n_groups = (n_live + SUBLANES - 1) // SUBLANES
        scatter_start(n_groups, slot)
        st_ref[1 + slot] = n_groups
        st_ref[0] = 1 - slot
        return carry

    sub_block(0, 0, first=True)
    lax.fori_loop(1, n_sub, sub_block, 0)

    @pl.when(e == ne - 1)
    def _():
        scatter_wait(st_ref[1], 0)
        scatter_wait(st_ref[2], 1)


def _experts(starts, counts, xs, w_gate, w_up, w_down, n_tok):
    r, dw = xs.shape
    dh = dw - LANES
    ne, d, f = w_gate.shape
    rb = MOE_RB
    wsel = lambda e, st, ct: (e, 0, 0)
    grid_spec = pltpu.PrefetchScalarGridSpec(
        num_scalar_prefetch=2,
        grid=(ne,),
        in_specs=[
            pl.BlockSpec(memory_space=pl.ANY),
            pl.BlockSpec((1, d, f), wsel),
            pl.BlockSpec((1, d, f), wsel),
            pl.BlockSpec(memory_space=pl.ANY),
        ],
        out_specs=pl.BlockSpec(memory_space=pl.ANY),
        scratch_shapes=[
            pltpu.VMEM((d, f), BF16), pltpu.VMEM((d, f), BF16), pltpu.VMEM((f, d), BF16),
            pltpu.VMEM((2, f, d), F32),
            pltpu.VMEM((2, rb, dw), U32), pltpu.VMEM((2, rb, dh), U32),
            pltpu.VMEM((SUBLANES, rb), I32), pltpu.SMEM((rb,), I32),
            pltpu.SemaphoreType.DMA((2,)), pltpu.SemaphoreType.DMA((2,)), pltpu.SemaphoreType.DMA(()),
            pltpu.SemaphoreType.DMA((2,)),
            pltpu.SMEM((3,), I32),
        ],
    )
    return pl.pallas_call(
        _expert_kernel,
        grid_spec=grid_spec,
        out_shape=jax.ShapeDtypeStruct((TOP_K * n_tok + 2 * rb, dh), U32),
        compiler_params=_cparams(("arbitrary",)),
        name="moe_experts",
    )(starts, counts, xs, w_gate, w_up, w_down)


def _combine_kernel(*refs):
    y_refs = refs[:TOP_K]
    h_ref, wg_ref, wu_ref, wd_ref, g_ref, b_ref, o_ref, wgb_ref, wub_ref, wdb_ref = refs[TOP_K:]

    @pl.when(pl.program_id(0) == 0)
    def _():
        wgb_ref[...] = wg_ref[...].astype(BF16)
        wub_ref[...] = wu_ref[...].astype(BF16)
        wdb_ref[...] = wd_ref[...].astype(BF16)

    h = h_ref[...]
    hb = h.astype(BF16)
    gate = jnp.dot(hb, wgb_ref[...], preferred_element_type=F32)
    up = jnp.dot(hb, wub_ref[...], preferred_element_type=F32)
    hdn = (_silu(gate) * up).astype(BF16)
    shared = jnp.dot(hdn, wdb_ref[...], preferred_element_type=F32)

    r_lo, r_hi = _unpack_rows(y_refs[0][...])
    for k in range(1, TOP_K):
        lo, hi = _unpack_rows(y_refs[k][...])
        r_lo = r_lo + lo
        r_hi = r_hi + hi
    ff = jnp.concatenate([r_lo, r_hi], axis=1) + shared
    o_ref[...] = _layer_norm(ALPHA * h + ff, g_ref[...], b_ref[...])


def _combine(y, h, ws_gate, ws_up, ws_down, g, b):
    s, d = h.shape
    f = ws_gate.shape[1]
    tt = COMBINE_TT
    row = pl.BlockSpec((tt, d), lambda i: (i, 0))
    vec = pl.BlockSpec((1, d), lambda i: (0, 0))
    planes = [pl.BlockSpec((tt, d // 2), functools.partial(lambda k, i: (k * (s // tt) + i, 0), k))
              for k in range(TOP_K)]
    return pl.pallas_call(
        _combine_kernel,
        grid=(s // tt,),
        in_specs=planes + [
            row,
            pl.BlockSpec((d, f), lambda i: (0, 0)),
            pl.BlockSpec((d, f), lambda i: (0, 0)),
            pl.BlockSpec((f, d), lambda i: (0, 0)),
            vec, vec,
        ],
        out_specs=row,
        out_shape=jax.ShapeDtypeStruct((s, d), F32),
        scratch_shapes=[pltpu.VMEM((d, f), BF16), pltpu.VMEM((d, f), BF16), pltpu.VMEM((f, d), BF16)],
        compiler_params=_cparams(("arbitrary",)),
        name="moe_combine_ln2",
    )(*([y] * TOP_K), h, ws_gate, ws_up, ws_down, g, b)


def _rope_tables(positions):
    half = DA_QK_DIM // 2
    inv = ROPE_THETA ** (-jnp.arange(0, DA_QK_DIM, 2, dtype=F32) / DA_QK_DIM)
    ang = positions.astype(F32)[:, None] * inv
    cos = jnp.tile(jnp.cos(ang), (1, LANES // half))
    sin = jnp.sin(ang)
    sin = jnp.tile(jnp.concatenate([-sin, sin], axis=-1), (1, LANES // DA_QK_DIM))
    return cos, sin


def _moe(h, w_router, e_bias, w_gate, w_up, w_down, ws_gate, ws_up, ws_down, ln_g, ln_b):
    s, d = h.shape
    e = w_router.shape[1]
    idx, wt, rank, counts = _router(h, w_router, e_bias.reshape(1, e))
    counts = counts[0].astype(jnp.int32)
    acounts = (counts + ROW_ALIGN - 1) // ROW_ALIGN * ROW_ALIGN
    ends = jnp.cumsum(acounts)
    starts = (ends - acounts).astype(jnp.int32)
    dest_t = _dest(idx, rank, starts.astype(F32).reshape(1, e))
    n_rows = s * TOP_K + e * ROW_ALIGN + MOE_RB

    xs = _dispatch(dest_t, h, idx, wt, n_rows)
    y = _experts(starts, counts, xs, w_gate, w_up, w_down, s)
    return _combine(y, h, ws_gate, ws_up, ws_down, ln_g.reshape(1, d), ln_b.reshape(1, d))


def kernel(x, positions, ln_in_g, ln_in_b, w_in, w_out, lam_q1, lam_k1, lam_q2, lam_k2, da_norm_g,
           hg_lb_logits, hg_norm_g, ln1_g, ln1_b, w_router, e_bias, w_gate, w_up, w_down,
           ws_gate, ws_up, ws_down, ln2_g, ln2_b):
    bsz, s, d = x.shape
    assert bsz == 1 and w_in.shape[0] == DEPTH
    x2 = x.reshape(s, d)
    cos, sin = _rope_tables(positions.reshape(s))
    lower_bounds = jnp.cumsum(jax.nn.softmax(hg_lb_logits.astype(F32), axis=0), axis=0)
    lam = (jnp.exp(jnp.sum(lam_q1[0] * lam_k1[0])) - jnp.exp(jnp.sum(lam_q2[0] * lam_k2[0]))
           + LAMBDA_INIT).reshape(1).astype(F32)
    gi, bi = ln_in_g.reshape(1, d), ln_in_b.reshape(1, d)

    da_w = DA_HEADS * LANES
    hb = _ln_in(x2, gi, bi)
    w_in_b = w_in[0].astype(BF16)
    w_out_b = w_out[0].astype(BF16)
    qm = _proj("q", hb, w_in_b, 0, da_w, cos, sin)
    kr = _proj("k", hb, w_in_b, da_w, da_w, cos, sin)
    vb = _proj("v", hb, w_in_b, 2 * da_w, da_w)
    proj_h = _proj("plain", hb, w_in_b, 3 * da_w, w_in.shape[2] - 3 * da_w)
    y_a = _attention(lam, qm, kr, vb, da_norm_g[0].reshape(1, LANES))
    y_b = _hgrn(proj_h, lower_bounds[0].reshape(1, -1), hg_norm_g[0].reshape(1, LANES), 0)
    h1 = _outproj(y_a, y_b, w_out_b, x2, gi, bi, ln1_g[0].reshape(1, d), ln1_b[0].reshape(1, d))
    h2 = _moe(h1, w_router[0], e_bias[0], w_gate[0], w_up[0], w_down[0],
              ws_gate[0], ws_up[0], ws_down[0], ln2_g[0], ln2_b[0])
    return h2.reshape(bsz, s, d)
```

```python
import functools
import math

import jax
import jax.numpy as jnp
import numpy as np
from jax import lax
from jax.experimental import pallas as pl
from jax.experimental.pallas import tpu as pltpu

F32 = jnp.float32
BF16 = jnp.bfloat16
U32 = jnp.uint32
I32 = jnp.int32
HI_MASK = np.uint32(0xFFFF0000)

CHUNK = 64
DA_HEADS = 8
DA_QK_DIM = 64
DA_V_DIM = 2 * DA_QK_DIM
HG_HEADS = 8
HG_DIM = 128
ROPE_THETA = 10000.0
N_EXPERTS = 256
TOP_K = 8
N_GROUPS = 8
TOPK_GROUPS = 4
GROUP_SIZE = N_EXPERTS // N_GROUPS
ROUTED_SCALE = 2.5
DEPTH = 1
ALPHA = (2.0 * DEPTH) ** 0.25
LN_EPS = 1e-5
RMS_EPS = 1e-5
LAMBDA_INIT = 0.8 - 0.6 * math.exp(-0.3 * 0)

LANES = 128
VMEM_LIMIT = 56 * 1024 * 1024
NEG = -1e30
DMA_PRIORITIES = 2

LN_TM = 512
INPROJ_TM, INPROJ_TN = 1024, 512
ATTN_T = 1024
HG_TR = 512
HG_SUB = 16
HG_GRP = 128
OUT_TM, OUT_TK = 512, 1024
ROUTER_TM = 512
MOE_RB = 256
MOE_CH = 64
SUBLANES = 8
ROW_ALIGN = SUBLANES
TAG_EXPERT0 = 8
DISPATCH_TT = 256
COMBINE_TT = 128


def _cparams(sem):
    return pltpu.CompilerParams(dimension_semantics=sem, vmem_limit_bytes=VMEM_LIMIT)


def _layer_norm(xf, g, b):
    mu = jnp.mean(xf, axis=-1, keepdims=True)
    xc = xf - mu
    var = jnp.mean(xc * xc, axis=-1, keepdims=True)
    return xc * lax.rsqrt(var + LN_EPS) * g + b


def _silu(x):
    return x * jax.nn.sigmoid(x)


def _ln_in_kernel(x_ref, g_ref, b_ref, o_ref):
    o_ref[...] = _layer_norm(x_ref[...], g_ref[...], b_ref[...]).astype(o_ref.dtype)


def _ln_in(x, g, b):
    s, d = x.shape
    tm = LN_TM
    vec = pl.BlockSpec((1, d), lambda i: (0, 0))
    return pl.pallas_call(
        _ln_in_kernel,
        grid=(s // tm,),
        in_specs=[pl.BlockSpec((tm, d), lambda i: (i, 0)), vec, vec],
        out_specs=pl.BlockSpec((tm, d), lambda i: (i, 0)),
        out_shape=jax.ShapeDtypeStruct((s, d), BF16),
        compiler_params=_cparams(("arbitrary",)),
        name="ln_in",
    )(x, g, b)


def _rope(x, cos, sin, first_half):
    rot = jnp.where(first_half, pltpu.roll(x, LANES - DA_QK_DIM // 2, 1), pltpu.roll(x, DA_QK_DIM // 2, 1))
    return x * cos + rot * sin


def _proj_kernel(role, h_ref, w_ref, *rest):
    acc = jnp.dot(h_ref[...], w_ref[...], preferred_element_type=F32)
    if role == "plain":
        (o_ref,) = rest
        o_ref[...] = acc
        return
    heads = acc.shape[1] // LANES
    if role == "v":
        (o_ref,) = rest
        for hh in range(heads):
            o_ref[hh] = acc[:, hh * LANES:(hh + 1) * LANES].astype(BF16)
        return
    cos_ref, sin_ref, o_ref = rest
    cos = cos_ref[...]
    sin = sin_ref[...]
    lane = lax.broadcasted_iota(jnp.int32, cos.shape, 1)
    first_half = (lane % DA_QK_DIM) < (DA_QK_DIM // 2)
    map0 = lane < DA_QK_DIM
    for hh in range(heads):
        r = _rope(acc[:, hh * LANES:(hh + 1) * LANES], cos, sin, first_half)
        if role == "k":
            o_ref[hh] = r.astype(BF16)
        else:
            r = r * (DA_QK_DIM ** -0.5)
            o_ref[hh, 0] = jnp.where(map0, r, 0.0).astype(BF16)
            o_ref[hh, 1] = jnp.where(map0, 0.0, r).astype(BF16)


def _proj(role, hb, w, col0, ncols, cos=None, sin=None):
    s, d = hb.shape
    tm, tn = INPROJ_TM, INPROJ_TN
    jb = col0 // tn
    hpt = tn // LANES
    nh = ncols // LANES
    in_specs = [pl.BlockSpec((tm, d), lambda i, j: (i, 0)), pl.BlockSpec((d, tn), lambda i, j: (0, jb + j))]
    args = [hb, w]
    if role in ("q", "k"):
        tab = pl.BlockSpec((tm, LANES), lambda i, j: (i, 0))
        in_specs += [tab, tab]
        args += [cos, sin]
    if role == "plain":
        out_spec = pl.BlockSpec((tm, tn), lambda i, j: (i, j))
        out_shape = jax.ShapeDtypeStruct((s, ncols), F32)
    elif role == "q":
        out_spec = pl.BlockSpec((hpt, 2, tm, LANES), lambda i, j: (j, 0, i, 0))
        out_shape = jax.ShapeDtypeStruct((nh, 2, s, LANES), BF16)
    else:
        out_spec = pl.BlockSpec((hpt, tm, LANES), lambda i, j: (j, i, 0))
        out_shape = jax.ShapeDtypeStruct((nh, s, LANES), BF16)
    return pl.pallas_call(
        functools.partial(_proj_kernel, role),
        grid=(s // tm, ncols // tn),
        in_specs=in_specs,
        out_specs=out_spec,
        out_shape=out_shape,
        compiler_params=_cparams(("arbitrary", "arbitrary")),
        name="inproj_" + role,
    )(*args)


def _attn_kernel(lam_ref, q_ref, k_ref, v_ref, g_ref, o_ref, acc_ref, m_ref, l_ref):
    t = ATTN_T
    i = pl.program_id(1)
    hf = t // 2
    qa = q_ref[0]
    q = jnp.concatenate([qa[0, :hf], qa[1, :hf], qa[0, hf:], qa[1, hf:]], axis=0)
    m_ref[...] = jnp.full(m_ref.shape, NEG, F32)
    l_ref[...] = jnp.zeros(l_ref.shape, F32)
    acc_ref[...] = jnp.zeros(acc_ref.shape, F32)

    def step(start, width, rows, masked):
        k = k_ref[0, pl.ds(start, width), :]
        v = v_ref[0, pl.ds(start, width), :]
        s = lax.dot_general(q[rows], k, (((1,), (1,)), ((), ())), preferred_element_type=F32)
        if masked:
            row = lax.broadcasted_iota(jnp.int32, s.shape, 0) % hf
            col = lax.broadcasted_iota(jnp.int32, s.shape, 1)
            s = jnp.where((col // CHUNK) <= (row // CHUNK), s, NEG)
        cols = [s[:, j * LANES:(j + 1) * LANES] for j in range(width // LANES)]
        mx = cols[0]
        for cj in cols[1:]:
            mx = jnp.maximum(mx, cj)
        m_prev = m_ref[rows, :]
        m_new = jnp.maximum(m_prev, jnp.max(mx, axis=-1, keepdims=True))
        alpha = jnp.exp(m_prev - m_new)
        ps = [jnp.exp(cj - m_new) for cj in cols]
        lsum = ps[0]
        for pj in ps[1:]:
            lsum = lsum + pj
        l_ref[rows, :] = alpha * l_ref[rows, :] + lsum
        m_ref[rows, :] = m_new
        p = jnp.concatenate([pj.astype(BF16) for pj in ps], axis=-1)
        acc_ref[rows, :] = alpha * acc_ref[rows, :] + jnp.dot(p, v, preferred_element_type=F32)

    def body(c, carry):
        step(pl.multiple_of(c * t, t), t, slice(0, 2 * t), False)
        return carry

    lax.fori_loop(0, i, body, 0)
    d0 = pl.multiple_of(i * t, t)
    step(d0, hf, slice(0, t), True)
    step(d0, hf, slice(t, 2 * t), False)
    step(pl.multiple_of(d0 + hf, hf), hf, slice(t, 2 * t), True)

    lam = lam_ref[0]
    acc = acc_ref[...]
    l = jnp.sum(l_ref[...], axis=-1, keepdims=True)
    o1 = jnp.concatenate([acc[:hf] / l[:hf], acc[t:t + hf] / l[t:t + hf]], axis=0)
    o2 = jnp.concatenate([acc[hf:t] / l[hf:t], acc[t + hf:] / l[t + hf:]], axis=0)
    o = o1 - lam * o2
    o = o * lax.rsqrt(jnp.mean(o * o, axis=-1, keepdims=True) + RMS_EPS) * g_ref[...] * (1.0 - LAMBDA_INIT)
    o_ref[...] = o.astype(o_ref.dtype)


def _attention(lam, qm, kr, vb, norm_g):
    h, _, s, _ = qm.shape
    t = ATTN_T
    return pl.pallas_call(
        _attn_kernel,
        grid=(h, s // t),
        in_specs=[
            pl.BlockSpec(memory_space=pltpu.SMEM),
            pl.BlockSpec((1, 2, t, LANES), lambda hh, i: (hh, 0, i, 0)),
            pl.BlockSpec((1, s, LANES), lambda hh, i: (hh, 0, 0)),
            pl.BlockSpec((1, s, LANES), lambda hh, i: (hh, 0, 0)),
            pl.BlockSpec((1, LANES), lambda hh, i: (0, 0)),
        ],
        out_specs=pl.BlockSpec((t, LANES), lambda hh, i: (i, hh)),
        out_shape=jax.ShapeDtypeStruct((s, h * LANES), BF16),
        scratch_shapes=[
            pltpu.VMEM((2 * t, LANES), F32),
            pltpu.VMEM((2 * t, LANES), F32),
            pltpu.VMEM((2 * t, LANES), F32),
        ],
        compiler_params=_cparams(("arbitrary", "arbitrary")),
        name="diff_attn",
    )(lam, qm, kr, vb, norm_g)


def _hgrn_kernel(q_ref, f_ref, i_ref, g_ref, lb_ref, ng_ref, o_ref,
                 st_ref, qs_ref, kk_ref, b_ref, kh_ref, od_ref, iv_ref):
    tr, sub, grp = HG_TR, HG_SUB, HG_GRP

    @pl.when(pl.program_id(1) == 0)
    def _():
        st_ref[...] = jnp.zeros(st_ref.shape, F32)

    lb = lb_ref[...]
    f = lb + (1.0 - lb) * jax.nn.sigmoid(f_ref[...])
    log_f = jnp.log(f)
    kk = 1.0 - f
    qs_ref[...] = _silu(q_ref[...])
    zpad = jnp.zeros((sub, LANES), F32)
    kk_ref[0:sub, :] = zpad
    b_ref[0:sub, :] = zpad
    iv_ref[0:sub, :] = zpad
    kk_ref[sub:, :] = kk
    iv_ref[sub:, :] = i_ref[...]

    r = lax.broadcasted_iota(jnp.int32, (grp, grp), 0)
    c = lax.broadcasted_iota(jnp.int32, (grp, grp), 1)
    same = (r // sub) == (c // sub)
    tri = jnp.where(same & (c <= r), 1.0, 0.0).astype(F32)
    blk = jnp.where(same, 1.0, 0.0).astype(F32)
    grp_iota = lax.broadcasted_iota(jnp.int32, (grp, LANES), 0)
    pos_in_sub = grp_iota % sub

    for gi in range(tr // grp):
        g0 = gi * grp
        rows = slice(g0, g0 + grp)
        prow = slice(sub + g0, sub + g0 + grp)
        lf = log_f[rows]
        bg = jnp.dot(tri, lf, preferred_element_type=F32, precision=lax.Precision.HIGHEST)
        tot = jnp.dot(blk, lf, preferred_element_type=F32, precision=lax.Precision.HIGHEST)
        b_ref[prow, :] = bg
        kh_ref[rows, :] = kk[rows] * jnp.exp(tot - bg)
        qg = qs_ref[rows, :]
        acc = jnp.sum(qg * kk[rows], axis=-1, keepdims=True) * i_ref[rows, :]
        for lag in range(1, sub):
            srow = slice(sub + g0 - lag, sub + g0 - lag + grp)
            e = jnp.exp(jnp.where(pos_in_sub >= lag, bg - b_ref[srow, :], NEG))
            w = jnp.sum(qg * kk_ref[srow, :] * e, axis=-1, keepdims=True)
            acc = acc + w * iv_ref[srow, :]
        od_ref[rows, :] = acc

    st = st_ref[...]
    for gi in range(tr // grp):
        g0 = gi * grp
        iv_t = i_ref[g0:g0 + grp, :].T.astype(BF16)
        kh_g = kh_ref[g0:g0 + grp, :]
        upds = []
        for j in range(grp // sub):
            in_sub = (grp_iota >= j * sub) & (grp_iota < (j + 1) * sub)
            upds.append(jnp.dot(iv_t, jnp.where(in_sub, kh_g, 0.0).astype(BF16), preferred_element_type=F32))
        for j in range(grp // sub):
            base = g0 + j * sub
            bj = b_ref[sub + base:sub + base + sub, :]
            qj = qs_ref[base:base + sub, :]
            o_inter = lax.dot_general((qj * jnp.exp(bj)).astype(BF16), st.astype(BF16),
                                      (((1,), (1,)), ((), ())), preferred_element_type=F32)
            od_ref[base:base + sub, :] = od_ref[base:base + sub, :] + o_inter
            st = st * jnp.exp(bj[sub - 1:sub]) + upds[j]
    st_ref[...] = st

    o = od_ref[...]
    o = o * lax.rsqrt(jnp.mean(o * o, axis=-1, keepdims=True) + RMS_EPS) * ng_ref[...]
    o_ref[...] = (o * _silu(g_ref[...])).astype(o_ref.dtype)


def _hgrn(proj, lower_bound, norm_g, col0):
    s = proj.shape[0]
    tr, h = HG_TR, HG_HEADS
    blk = lambda off: pl.BlockSpec((tr, LANES), lambda hh, i: (i, col0 + off + hh))
    pad = pltpu.VMEM((tr + HG_SUB, LANES), F32)
    full = pltpu.VMEM((tr, LANES), F32)
    return pl.pallas_call(
        _hgrn_kernel,
        grid=(h, s // tr),
        in_specs=[blk(0), blk(h), blk(2 * h), blk(3 * h),
                  pl.BlockSpec((1, LANES), lambda hh, i: (0, hh)),
                  pl.BlockSpec((1, LANES), lambda hh, i: (0, 0))],
        out_specs=pl.BlockSpec((tr, LANES), lambda hh, i: (i, hh)),
        out_shape=jax.ShapeDtypeStruct((s, h * LANES), BF16),
        scratch_shapes=[pltpu.VMEM((HG_DIM, HG_DIM), F32), full, pad, pad, full, full, pad],
        compiler_params=_cparams(("arbitrary", "arbitrary")),
        name="hgrn2",
    )(proj, proj, proj, proj, lower_bound, norm_g)


def _outproj_kernel(ya_ref, yb_ref, w_ref, x_ref, gi_ref, bi_ref, g1_ref, b1_ref, h_ref, acc_ref):
    k = pl.program_id(1)
    nk = pl.num_programs(1)

    @pl.when(k == 0)
    def _():
        acc_ref[...] = jnp.zeros(acc_ref.shape, F32)

    w = w_ref[...]

    @pl.when(k < nk // 2)
    def _():
        acc_ref[...] += jnp.dot(ya_ref[...], w, preferred_element_type=F32)

    @pl.when(k >= nk // 2)
    def _():
        acc_ref[...] += jnp.dot(yb_ref[...], w, preferred_element_type=F32)

    @pl.when(k == nk - 1)
    def _():
        h0 = _layer_norm(x_ref[...], gi_ref[...], bi_ref[...])
        h_ref[...] = _layer_norm(ALPHA * h0 + acc_ref[...], g1_ref[...], b1_ref[...])


def _outproj(ya, yb, w, x, gi, bi, g1, b1):
    s, d = x.shape
    tm, tk = OUT_TM, OUT_TK
    nka = ya.shape[1] // tk
    nk = w.shape[0] // tk
    assert nk == 2 * nka and yb.shape[1] == ya.shape[1]
    vec = pl.BlockSpec((1, d), lambda i, k: (0, 0))
    return pl.pallas_call(
        _outproj_kernel,
        grid=(s // tm, nk),
        in_specs=[
            pl.BlockSpec((tm, tk), lambda i, k: (i, jnp.minimum(k, nka - 1))),
            pl.BlockSpec((tm, tk), lambda i, k: (i, jnp.maximum(k - nka, 0))),
            pl.BlockSpec((tk, d), lambda i, k: (k, 0)),
            pl.BlockSpec((tm, d), lambda i, k: (i, 0)),
            vec, vec, vec, vec,
        ],
        out_specs=pl.BlockSpec((tm, d), lambda i, k: (i, 0)),
        out_shape=jax.ShapeDtypeStruct((s, d), F32),
        scratch_shapes=[pltpu.VMEM((tm, d), F32)],
        compiler_params=_cparams(("arbitrary", "arbitrary")),
        name="outproj_ln1",
    )(ya, yb, w, x, gi, bi, g1, b1)


def _router_kernel(h_ref, w_ref, eb_ref, idx_ref, wt_ref, rank_ref, cnt_ref, carry_ref):
    tm = ROUTER_TM
    i = pl.program_id(0)

    @pl.when(i == 0)
    def _():
        carry_ref[...] = jnp.zeros(carry_ref.shape, F32)

    logits = jnp.dot(h_ref[...], w_ref[...], preferred_element_type=F32, precision=lax.Precision.HIGHEST)
    scores = jax.nn.sigmoid(logits)
    choice = scores + eb_ref[...]
    lane = lax.broadcasted_iota(jnp.int32, choice.shape, 1)
    grp_of_lane = lane // GROUP_SIZE
    big = jnp.int32(N_EXPERTS)
    ninf = -jnp.inf

    def first_argmax(vals):
        m = jnp.max(vals, axis=-1, keepdims=True)
        idx = jnp.min(jnp.where(vals == m, lane, big), axis=-1, keepdims=True)
        return m, idx

    gs = []
    for g in range(N_GROUPS):
        vals = jnp.where(grp_of_lane == g, choice, ninf)
        m1, i1 = first_argmax(vals)
        m2 = jnp.max(jnp.where(lane == i1, ninf, vals), axis=-1, keepdims=True)
        gs.append(m1 + m2)
    allowed = jnp.zeros(choice.shape, jnp.int32)
    for g in range(N_GROUPS):
        beats = jnp.zeros(gs[g].shape, jnp.int32)
        for g2 in range(N_GROUPS):
            if g2 == g:
                continue
            better = (gs[g2] >= gs[g]) if g2 < g else (gs[g2] > gs[g])
            beats = beats + jnp.where(better, 1, 0)
        allowed = jnp.where(grp_of_lane == g, jnp.where(beats < TOPK_GROUPS, 1, 0), allowed)
    masked = jnp.where(allowed > 0, choice, ninf)

    sel = jnp.zeros(choice.shape, F32)
    idxs, wts = [], []
    for _ in range(TOP_K):
        _, ik = first_argmax(masked)
        hit = lane == ik
        idxs.append(ik)
        wts.append(jnp.sum(jnp.where(hit, scores, 0.0), axis=-1, keepdims=True))
        sel = jnp.where(hit, 1.0, sel)
        masked = jnp.where(hit, ninf, masked)
    wsum = wts[0]
    for k in range(1, TOP_K):
        wsum = wsum + wts[k]

    r = lax.broadcasted_iota(jnp.int32, (tm, tm), 0)
    c = lax.broadcasted_iota(jnp.int32, (tm, tm), 1)
    strict_lower = jnp.where(c < r, 1.0, 0.0).astype(BF16)
    rank = jnp.dot(strict_lower, sel.astype(BF16), preferred_element_type=F32) + carry_ref[...]
    carry_ref[...] = carry_ref[...] + jnp.sum(sel, axis=0, keepdims=True)
    cnt_ref[...] = carry_ref[...]

    out_lane = lax.broadcasted_iota(jnp.int32, (tm, LANES), 1)
    idx_out = jnp.zeros((tm, LANES), jnp.int32)
    wt_out = jnp.zeros((tm, LANES), F32)
    rank_out = jnp.zeros((tm, LANES), jnp.int32)
    for k in range(TOP_K):
        rk = jnp.sum(jnp.where(lane == idxs[k], rank, 0.0), axis=-1, keepdims=True)
        idx_out = jnp.where(out_lane == k, idxs[k], idx_out)
        wt_out = jnp.where(out_lane == k, wts[k] / wsum * ROUTED_SCALE, wt_out)
        rank_out = jnp.where(out_lane == k, rk.astype(jnp.int32), rank_out)
    idx_ref[...] = idx_out
    wt_ref[...] = wt_out
    rank_ref[...] = rank_out


def _router(h, w_router, e_bias):
    s, d = h.shape
    e = w_router.shape[1]
    tm = ROUTER_TM
    row = pl.BlockSpec((tm, LANES), lambda i: (i, 0))
    return pl.pallas_call(
        _router_kernel,
        grid=(s // tm,),
        in_specs=[
            pl.BlockSpec((tm, d), lambda i: (i, 0)),
            pl.BlockSpec((d, e), lambda i: (0, 0)),
            pl.BlockSpec((1, e), lambda i: (0, 0)),
        ],
        out_specs=[row, row, row, pl.BlockSpec((1, e), lambda i: (0, 0))],
        out_shape=[
            jax.ShapeDtypeStruct((s, LANES), jnp.int32),
            jax.ShapeDtypeStruct((s, LANES), F32),
            jax.ShapeDtypeStruct((s, LANES), jnp.int32),
            jax.ShapeDtypeStruct((1, e), F32),
        ],
        scratch_shapes=[pltpu.VMEM((1, e), F32)],
        compiler_params=_cparams(("arbitrary",)),
        name="router_topk",
    )(h, w_router, e_bias)


def _dest_kernel(idx_ref, rank_ref, ps_ref, dest_ref):
    idx = idx_ref[...]
    rank = rank_ref[...]
    ps = ps_ref[...]
    lane = lax.broadcasted_iota(jnp.int32, (idx.shape[0], ps.shape[1]), 1)
    out_lane = lax.broadcasted_iota(jnp.int32, idx.shape, 1)
    dest = jnp.zeros(idx.shape, jnp.int32)
    for k in range(TOP_K):
        start_k = jnp.sum(jnp.where(lane == idx[:, k:k + 1], ps, 0.0), axis=-1, keepdims=True)
        dest = jnp.where(out_lane == k, start_k.astype(jnp.int32) + rank, dest)
    dest_ref[...] = dest.T[:TOP_K, :]


def _dest(idx, rank, pstarts):
    s = idx.shape[0]
    e = pstarts.shape[1]
    tm = ROUTER_TM
    row = pl.BlockSpec((tm, LANES), lambda i: (i, 0))
    return pl.pallas_call(
        _dest_kernel,
        grid=(s // tm,),
        in_specs=[row, row, pl.BlockSpec((1, e), lambda i: (0, 0))],
        out_specs=pl.BlockSpec((TOP_K, tm), lambda i: (0, i)),
        out_shape=jax.ShapeDtypeStruct((TOP_K, s), jnp.int32),
        compiler_params=_cparams(("arbitrary",)),
        name="moe_dest",
    )(idx, rank, pstarts)


def _pack_rows(x):
    half = x.shape[1] // 2
    lo = lax.bitcast_convert_type(x[:, :half].astype(BF16).astype(F32), U32) >> 16
    hi = lax.bitcast_convert_type(x[:, half:].astype(BF16).astype(F32), U32) & HI_MASK
    return lo | hi


def _unpack_rows(w):
    lo = lax.bitcast_convert_type(w << 16, F32)
    hi = lax.bitcast_convert_type(w & HI_MASK, F32)
    return lo, hi


def _dispatch_kernel(dest_ref, h_ref, idx_ref, wt_ref, xs_ref, hp_ref, sem):
    tt = DISPATCH_TT
    dh = h_ref.shape[1] // 2
    hp_ref[:, :dh] = _pack_rows(h_ref[...])
    lane = lax.broadcasted_iota(I32, (tt, LANES), 1)
    tok = lax.broadcasted_iota(I32, (tt, LANES), 0) + pl.program_id(0) * tt
    experts = pltpu.roll(idx_ref[...], TAG_EXPERT0, 1)
    weights = pltpu.roll(lax.bitcast_convert_type(wt_ref[...], I32), TAG_EXPERT0 + TOP_K, 1)
    tag = jnp.where(lane == 0, tok,
                    jnp.where((lane >= TAG_EXPERT0) & (lane < TAG_EXPERT0 + TOP_K), experts,
                              jnp.where((lane >= TAG_EXPERT0 + TOP_K) & (lane < TAG_EXPERT0 + 2 * TOP_K), weights, 0)))
    hp_ref[:, dh:] = lax.bitcast_convert_type(tag, U32)

    def row_copy(t, k):
        return pltpu.make_async_copy(hp_ref.at[pl.ds(t, 1)], xs_ref.at[pl.ds(dest_ref[k, t], 1)], sem)

    def drain(t, carry):
        for k in range(TOP_K):
            row_copy(t, k).wait()
        return carry

    for t in range(tt):
        for k in range(TOP_K):
            row_copy(t, k).start(priority=(t * TOP_K + k) % DMA_PRIORITIES)
    lax.fori_loop(0, tt, drain, 0)


def _dispatch(dest_t, h, idx, wt, n_rows):
    s, d = h.shape
    tt = DISPATCH_TT
    row = pl.BlockSpec((tt, LANES), lambda i: (i, 0))
    return pl.pallas_call(
        _dispatch_kernel,
        grid=(s // tt,),
        in_specs=[
            pl.BlockSpec((TOP_K, tt), lambda i: (0, i), memory_space=pltpu.SMEM),
            pl.BlockSpec((tt, d), lambda i: (i, 0)),
            row, row,
        ],
        out_specs=pl.BlockSpec(memory_space=pl.ANY),
        out_shape=jax.ShapeDtypeStruct((n_rows, d // 2 + LANES), U32),
        scratch_shapes=[pltpu.VMEM((tt, d // 2 + LANES), U32), pltpu.SemaphoreType.DMA(())],
        compiler_params=_cparams(("arbitrary",)),
        name="moe_dispatch",
    )(dest_t, h, idx, wt)


def _expert_kernel(start_ref, cnt_ref, xs_ref, wg_ref, wu_ref, wd_ref, y_ref,
                   wgb_ref, wub_ref, wdb_ref, wd_buf_ref, xin_ref, yout_ref, slot_v_ref, slot_s_ref, w_ref,
                   in_sem, out_sem, slot_sem, wd_sem, st_ref):
    e = pl.program_id(0)
    ne = pl.num_programs(0)
    rb, ch = MOE_RB, MOE_CH
    dh = yout_ref.shape[2]
    n_tok = (y_ref.shape[0] - 2 * rb) // TOP_K
    wd_slot = e % 2

    def wd_copy(expert, slot):
        return pltpu.make_async_copy(wd_ref.at[expert], wd_buf_ref.at[slot], wd_sem.at[slot])

    @pl.when(e == 0)
    def _():
        wd_copy(0, 0).start(priority=DMA_PRIORITIES - 1)

    @pl.when(e + 1 < ne)
    def _():
        wd_copy(e + 1, 1 - wd_slot).start(priority=DMA_PRIORITIES - 1)

    def n_chunks(cnt_e, j):
        return jnp.clip((cnt_e - j * rb + ch - 1) // ch, 0, rb // ch)

    def in_chunk(row0, slot, c):
        return pltpu.make_async_copy(xs_ref.at[pl.ds(pl.multiple_of(row0 + c * ch, ROW_ALIGN), ch)],
                                     xin_ref.at[slot, pl.ds(pl.multiple_of(c * ch, ch), ch)], in_sem.at[slot])

    def out_row(slot, r, dst):
        return pltpu.make_async_copy(yout_ref.at[slot, pl.ds(r, 1)], y_ref.at[pl.ds(dst, 1)], out_sem.at[slot])

    def scatter_start(n_groups, slot):
        for g in range(rb // SUBLANES):
            @pl.when(g < n_groups)
            def _():
                for u in range(SUBLANES):
                    r = g * SUBLANES + u
                    out_row(slot, r, slot_s_ref[r]).start(priority=DMA_PRIORITIES - 1)

    def scatter_wait(n_groups, slot):
        def body(g, carry):
            for u in range(SUBLANES):
                out_row(slot, 0, 0).wait()
            return carry
        lax.fori_loop(0, n_groups, body, 0)

    def for_chunks(n, fn):
        def body(c, carry):
            fn(c)
            return carry
        lax.fori_loop(0, n, body, 0)

    lane = lax.broadcasted_iota(I32, (rb, LANES), 1)

    def slot_copy():
        return pltpu.make_async_copy(slot_v_ref.at[0], slot_s_ref, slot_sem)

    def prepare(slot, n_wait, row0, expert, n_live):
        for_chunks(n_wait, lambda c: in_chunk(row0, slot, c).wait())
        tag = lax.bitcast_convert_type(xin_ref[slot, :, dh:], I32)
        mine = (tag == expert) & (lane >= TAG_EXPERT0) & (lane < TAG_EXPERT0 + TOP_K)
        k_sel = jnp.sum(jnp.where(mine, lane - TAG_EXPERT0, 0).astype(F32), axis=-1, keepdims=True)
        w_sel = jnp.sum(jnp.where(pltpu.roll(jnp.where(mine, 1, 0), TOP_K, 1) > 0,
                                  lax.bitcast_convert_type(tag, F32), 0.0), axis=-1, keepdims=True)
        tok = jnp.sum(jnp.where(lane == 0, tag, 0).astype(F32), axis=-1, keepdims=True)
        row_i = lax.broadcasted_iota(I32, (rb, 1), 0)
        dst = jnp.where(row_i < n_live, (k_sel * n_tok + tok).astype(I32), TOP_K * n_tok + slot * rb + row_i)
        for g in range(rb // LANES):
            blk = jnp.broadcast_to(dst[g * LANES:(g + 1) * LANES], (LANES, LANES))
            slot_v_ref[:, g * LANES:(g + 1) * LANES] = blk.T[0:SUBLANES, :]
        w_ref[...] = jnp.broadcast_to(w_sel, (rb, LANES))
        slot_copy().start()

    @pl.when(e == 0)
    def _():
        st_ref[0] = 0
        st_ref[1] = 0
        st_ref[2] = 0
        for_chunks(n_chunks(cnt_ref[0], 0), lambda c: in_chunk(start_ref[0], 0, c).start())
        prepare(0, n_chunks(cnt_ref[0], 0), start_ref[0], 0, jnp.clip(cnt_ref[0], 0, rb))

    start = start_ref[e]
    cnt = cnt_ref[e]
    n_sub = jnp.maximum(1, (cnt + rb - 1) // rb)
    e_next = jnp.minimum(e + 1, ne - 1)
    next_start = start_ref[e_next]
    next_cnt = cnt_ref[e_next]
    next_first = jnp.where(e == ne - 1, 0, n_chunks(next_cnt, 0))

    def sub_block(j, carry, first=False):
        slot = st_ref[0]
        row0 = start + j * rb
        n_cur = n_chunks(cnt, j)
        n_live = jnp.clip(cnt - j * rb, 0, rb)
        last_j = j == n_sub - 1
        nxt_row = jnp.where(last_j, next_start, row0 + rb)
        n_nxt = jnp.where(last_j, next_first, n_chunks(cnt, j + 1))
        for_chunks(n_nxt, lambda c: in_chunk(nxt_row, 1 - slot, c).start())

        xw = xin_ref[slot, :, :dh]
        live = (lax.broadcasted_iota(I32, xw.shape, 0) + j * rb) < cnt
        lo, hi = _unpack_rows(jnp.where(live, xw, jnp.zeros_like(xw)))
        xb = jnp.concatenate([lo.astype(BF16), hi.astype(BF16)], axis=1)
        if first:
            wgb_ref[...] = wg_ref[0].astype(BF16)
        gate = jnp.dot(xb, wgb_ref[...], preferred_element_type=F32)
        if first:
            wub_ref[...] = wu_ref[0].astype(BF16)
        up = jnp.dot(xb, wub_ref[...], preferred_element_type=F32)
        if first:
            wd_copy(e, wd_slot).wait()
            wdb_ref[...] = wd_buf_ref[wd_slot].astype(BF16)
        hdn = (_silu(gate) * up).astype(BF16)
        y = jnp.dot(hdn, wdb_ref[...], preferred_element_type=F32) * w_ref[:, 0:1]

        scatter_wait(st_ref[1 + slot], slot)
        yout_ref[slot] = _pack_rows(y)
        slot_copy().wait()
        n_groups = (n_live + SUBLANES - 1) // SUBLANES
        scatter_start(n_groups, slot)
        st_ref[1 + slot] = n_groups
        st_ref[0] = 1 - slot

        @pl.when(jnp.logical_not(last_j & (e == ne - 1)))
        def _():
            prepare(1 - slot, n_nxt, nxt_row, jnp.where(last_j, e_next, e),
                    jnp.where(last_j, jnp.clip(next_cnt, 0, rb), jnp.clip(cnt - (j + 1) * rb, 0, rb)))
        return carry

    sub_block(0, 0, first=True)
    lax.fori_loop(1, n_sub, sub_block, 0)

    @pl.when(e == ne - 1)
    def _():
        scatter_wait(st_ref[1], 0)
        scatter_wait(st_ref[2], 1)


def _experts(starts, counts, xs, w_gate, w_up, w_down, n_tok):
    r, dw = xs.shape
    dh = dw - LANES
    ne, d, f = w_gate.shape
    rb = MOE_RB
    wsel = lambda e, st, ct: (e, 0, 0)
    grid_spec = pltpu.PrefetchScalarGridSpec(
        num_scalar_prefetch=2,
        grid=(ne,),
        in_specs=[
            pl.BlockSpec(memory_space=pl.ANY),
            pl.BlockSpec((1, d, f), wsel),
            pl.BlockSpec((1, d, f), wsel),
            pl.BlockSpec(memory_space=pl.ANY),
        ],
        out_specs=pl.BlockSpec(memory_space=pl.ANY),
        scratch_shapes=[
            pltpu.VMEM((d, f), BF16), pltpu.VMEM((d, f), BF16), pltpu.VMEM((f, d), BF16),
            pltpu.VMEM((2, f, d), F32),
            pltpu.VMEM((2, rb, dw), U32), pltpu.VMEM((2, rb, dh), U32),
            pltpu.VMEM((SUBLANES, rb), I32), pltpu.SMEM((rb,), I32), pltpu.VMEM((rb, LANES), F32),
            pltpu.SemaphoreType.DMA((2,)), pltpu.SemaphoreType.DMA((2,)), pltpu.SemaphoreType.DMA(()),
            pltpu.SemaphoreType.DMA((2,)),
            pltpu.SMEM((3,), I32),
        ],
    )
    return pl.pallas_call(
        _expert_kernel,
        grid_spec=grid_spec,
        out_shape=jax.ShapeDtypeStruct((TOP_K * n_tok + 2 * rb, dh), U32),
        compiler_params=_cparams(("arbitrary",)),
        name="moe_experts",
    )(starts, counts, xs, w_gate, w_up, w_down)


def _combine_kernel(*refs):
    y_refs = refs[:TOP_K]
    h_ref, wg_ref, wu_ref, wd_ref, g_ref, b_ref, o_ref, wgb_ref, wub_ref, wdb_ref = refs[TOP_K:]

    @pl.when(pl.program_id(0) == 0)
    def _():
        wgb_ref[...] = wg_ref[...].astype(BF16)
        wub_ref[...] = wu_ref[...].astype(BF16)
        wdb_ref[...] = wd_ref[...].astype(BF16)

    h = h_ref[...]
    hb = h.astype(BF16)
    gate = jnp.dot(hb, wgb_ref[...], preferred_element_type=F32)
    up = jnp.dot(hb, wub_ref[...], preferred_element_type=F32)
    hdn = (_silu(gate) * up).astype(BF16)
    shared = jnp.dot(hdn, wdb_ref[...], preferred_element_type=F32)

    r_lo, r_hi = _unpack_rows(y_refs[0][...])
    for k in range(1, TOP_K):
        lo, hi = _unpack_rows(y_refs[k][...])
        r_lo = r_lo + lo
        r_hi = r_hi + hi
    ff = jnp.concatenate([r_lo, r_hi], axis=1) + shared
    o_ref[...] = _layer_norm(ALPHA * h + ff, g_ref[...], b_ref[...])


def _combine(y, h, ws_gate, ws_up, ws_down, g, b):
    s, d = h.shape
    f = ws_gate.shape[1]
    tt = COMBINE_TT
    row = pl.BlockSpec((tt, d), lambda i: (i, 0))
    vec = pl.BlockSpec((1, d), lambda i: (0, 0))
    planes = [pl.BlockSpec((tt, d // 2), functools.partial(lambda k, i: (k * (s // tt) + i, 0), k))
              for k in range(TOP_K)]
    return pl.pallas_call(
        _combine_kernel,
        grid=(s // tt,),
        in_specs=planes + [
            row,
            pl.BlockSpec((d, f), lambda i: (0, 0)),
            pl.BlockSpec((d, f), lambda i: (0, 0)),
            pl.BlockSpec((f, d), lambda i: (0, 0)),
            vec, vec,
        ],
        out_specs=row,
        out_shape=jax.ShapeDtypeStruct((s, d), F32),
        scratch_shapes=[pltpu.VMEM((d, f), BF16), pltpu.VMEM((d, f), BF16), pltpu.VMEM((f, d), BF16)],
        compiler_params=_cparams(("arbitrary",)),
        name="moe_combine_ln2",
    )(*([y] * TOP_K), h, ws_gate, ws_up, ws_down, g, b)


def _rope_tables(positions):
    half = DA_QK_DIM // 2
    inv = ROPE_THETA ** (-jnp.arange(0, DA_QK_DIM, 2, dtype=F32) / DA_QK_DIM)
    ang = positions.astype(F32)[:, None] * inv
    cos = jnp.tile(jnp.cos(ang), (1, LANES // half))
    sin = jnp.sin(ang)
    sin = jnp.tile(jnp.concatenate([-sin, sin], axis=-1), (1, LANES // DA_QK_DIM))
    return cos, sin


def _moe(h, w_router, e_bias, w_gate, w_up, w_down, ws_gate, ws_up, ws_down, ln_g, ln_b):
    s, d = h.shape
    e = w_router.shape[1]
    idx, wt, rank, counts = _router(h, w_router, e_bias.reshape(1, e))
    counts = counts[0].astype(jnp.int32)
    acounts = (counts + ROW_ALIGN - 1) // ROW_ALIGN * ROW_ALIGN
    ends = jnp.cumsum(acounts)
    starts = (ends - acounts).astype(jnp.int32)
    dest_t = _dest(idx, rank, starts.astype(F32).reshape(1, e))
    n_rows = s * TOP_K + e * ROW_ALIGN + MOE_RB

    xs = _dispatch(dest_t, h, idx, wt, n_rows)
    y = _experts(starts, counts, xs, w_gate, w_up, w_down, s)
    return _combine(y, h, ws_gate, ws_up, ws_down, ln_g.reshape(1, d), ln_b.reshape(1, d))


def kernel(x, positions, ln_in_g, ln_in_b, w_in, w_out, lam_q1, lam_k1, lam_q2, lam_k2, da_norm_g,
           hg_lb_logits, hg_norm_g, ln1_g, ln1_b, w_router, e_bias, w_gate, w_up, w_down,
           ws_gate, ws_up, ws_down, ln2_g, ln2_b):
    bsz, s, d = x.shape
    assert bsz == 1 and w_in.shape[0] == DEPTH
    x2 = x.reshape(s, d)
    cos, sin = _rope_tables(positions.reshape(s))
    lower_bounds = jnp.cumsum(jax.nn.softmax(hg_lb_logits.astype(F32), axis=0), axis=0)
    lam = (jnp.exp(jnp.sum(lam_q1[0] * lam_k1[0])) - jnp.exp(jnp.sum(lam_q2[0] * lam_k2[0]))
           + LAMBDA_INIT).reshape(1).astype(F32)
    gi, bi = ln_in_g.reshape(1, d), ln_in_b.reshape(1, d)

    da_w = DA_HEADS * LANES
    hb = _ln_in(x2, gi, bi)
    w_in_b = w_in[0].astype(BF16)
    w_out_b = w_out[0].astype(BF16)
    qm = _proj("q", hb, w_in_b, 0, da_w, cos, sin)
    kr = _proj("k", hb, w_in_b, da_w, da_w, cos, sin)
    vb = _proj("v", hb, w_in_b, 2 * da_w, da_w)
    proj_h = _proj("plain", hb, w_in_b, 3 * da_w, w_in.shape[2] - 3 * da_w)
    y_a = _attention(lam, qm, kr, vb, da_norm_g[0].reshape(1, LANES))
    y_b = _hgrn(proj_h, lower_bounds[0].reshape(1, -1), hg_norm_g[0].reshape(1, LANES), 0)
    h1 = _outproj(y_a, y_b, w_out_b, x2, gi, bi, ln1_g[0].reshape(1, d), ln1_b[0].reshape(1, d))
    h2 = _moe(h1, w_router[0], e_bias[0], w_gate[0], w_up[0], w_down[0],
              ws_gate[0], ws_up[0], ws_down[0], ln2_g[0], ln2_b[0])
    return h2.reshape(bsz, s, d)
```
